```python
import math
import jax, jax.numpy as jnp
from jax import lax
import numpy as np

D_MODEL = 1024
BATCH = 2
SEQ = 8192
DEPTH = 2
DEC_BATCH = 128
DEC_SEQ = 4
PAST_LEN = 2048
PAGE_SIZE = 128

N_META = 16
H_A = 4
KV_A = 2
G_A = H_A // KV_A
HD_A = 64
VD_A = 2 * HD_A
H_B = 8
KV_B = 2
G_B = H_B // KV_B
HD_B = 64
H_I = 8
D_I = 64
TOPK_MAX = 256
D_C = 512
CONV_C = 31
D_FF = 2816
CONV_F = 3

ROPE_THETA = 500000.0
ROT_FRAC_DEN = 4
Q_BLOCK = 128
EPS = 1e-6
LN_EPS = 1e-5

SPLIT_SIZES = (H_A * 2 * HD_A, KV_A * 2 * HD_A, KV_A * VD_A,
               H_B * HD_B, KV_B * HD_B, KV_B * HD_B,
               H_I * D_I, H_I, D_I,
               2 * D_C,
               3 * D_MODEL)
N_IN = sum(SPLIT_SIZES)

kernel_name = 'hybrid_diffattn_dsa_conformer_convffn_step'


def rms_norm(x, g):
    xf = x.astype(jnp.float32)
    y = xf * lax.rsqrt(jnp.mean(xf * xf, axis=-1, keepdims=True) + EPS)
    return (y * g.astype(jnp.float32)).astype(x.dtype)


def layer_norm(x, g, b):
    xf = x.astype(jnp.float32)
    mu = jnp.mean(xf, axis=-1, keepdims=True)
    xc = xf - mu
    y = xc * lax.rsqrt(jnp.mean(xc * xc, axis=-1, keepdims=True) + LN_EPS)
    return (y * g.astype(jnp.float32) + b.astype(jnp.float32)).astype(x.dtype)


def split_cols(z):
    offsets = [int(o) for o in np.cumsum(SPLIT_SIZES)[:-1]]
    return jnp.split(z, offsets, axis=-1)


def rotary(x, pos):
    dh = x.shape[-1]
    rot = dh // ROT_FRAC_DEN
    half = rot // 2
    inv = jnp.power(ROPE_THETA, -jnp.arange(half, dtype=jnp.float32) * 2.0 / rot)
    ang = pos.astype(jnp.float32)[:, None] * inv[None, :]
    cos = jnp.cos(ang)[:, None, :].astype(x.dtype)
    sin = jnp.sin(ang)[:, None, :].astype(x.dtype)
    x1 = x[..., :half]
    x2 = x[..., half:rot]
    return jnp.concatenate([x1 * cos - x2 * sin, x2 * cos + x1 * sin, x[..., rot:]], axis=-1)


def dwconv_causal(x_ext, w, b):
    c = x_ext.shape[-1]
    y = lax.conv_general_dilated(x_ext, w[:, None, :].astype(x_ext.dtype), window_strides=(1,),
                                 padding='VALID', dimension_numbers=('NWC', 'WIO', 'NWC'),
                                 feature_group_count=c)
    return y + b.astype(x_ext.dtype)


def diff_core(q, qpos, k, v, lam):
    s_len = k.shape[1]
    logits = jnp.einsum('bqkgmd,bskmd->bkgmqs', q, k).astype(jnp.float32) * (HD_A ** -0.5)
    vis = jnp.arange(s_len)[None, :] <= qpos[:, None]
    p = jax.nn.softmax(jnp.where(vis, logits, -jnp.inf), axis=-1)
    w = p[:, :, :, 0] - lam * p[:, :, :, 1]
    out = jnp.einsum('bkgqs,bske->bqkge', w.astype(v.dtype), v)
    return out.reshape(out.shape[0], out.shape[1], H_A, VD_A)


def dsa_core(qi, wi, q, qpos, ki, k, v, k_sel):
    bn, nq = q.shape[:2]
    s_len = ki.shape[1]
    dots = jnp.einsum('bqhd,bsd->bqhs', qi, ki).astype(jnp.float32) * (D_I ** -0.5)
    score = jnp.einsum('bqh,bqhs->bqs', wi.astype(jnp.float32), jax.nn.relu(dots))
    vis = jnp.arange(s_len)[None, None, :] <= qpos[None, :, None]
    score = jnp.where(vis, score, -jnp.inf)
    _, sel = lax.top_k(score, k_sel)
    take = jax.vmap(lambda rows, ids: rows[ids])
    ks = take(k, sel)
    vs = take(v, sel)
    qg = q.reshape(bn, nq, KV_B, G_B, HD_B)
    logits = jnp.einsum('bqkgd,bqskd->bqkgs', qg, ks).astype(jnp.float32) * (HD_B ** -0.5)
    ok = (sel <= qpos[None, :, None])[:, :, None, None, :]
    p = jax.nn.softmax(jnp.where(ok, logits, -jnp.inf), axis=-1)
    out = jnp.einsum('bqkgs,bqskd->bqkgd', p.astype(vs.dtype), vs)
    return out.reshape(bn, nq, H_B * HD_B)


def over_query_blocks(fn, qs, qpos):
    t_len = qpos.shape[0]
    nb = -(-t_len // Q_BLOCK)
    t_pad = nb * Q_BLOCK

    def to_blocks(a):
        a = jnp.pad(a, [(0, 0), (0, t_pad - t_len)] + [(0, 0)] * (a.ndim - 2))
        a = a.reshape((a.shape[0], nb, Q_BLOCK) + a.shape[2:])
        return jnp.moveaxis(a, 1, 0)

    pos_pad = jnp.concatenate([qpos, qpos[-1] + 1 + jnp.arange(t_pad - t_len, dtype=qpos.dtype)])
    blocks = tuple(to_blocks(a) for a in qs) + (pos_pad.reshape(nb, Q_BLOCK),)
    out = lax.map(lambda args: fn(*args), blocks)
    out = jnp.moveaxis(out, 0, 1)
    out = out.reshape((out.shape[0], t_pad) + out.shape[3:])
    return out[:, :t_len]


def trunk_layer(x, pos, l, p, prefix, blocked, k_sel):
    pre_kv_a, pre_kv_b, pre_idx, pre_conv_c, pre_conv_f = prefix
    bn, t_len, _ = x.shape
    xn = rms_norm(x, p['norm_mix'][l])
    z = xn @ p['w_in'][l]
    aq, ak, av, bq, bk, bv, iq, iw, ik, cin, gz = split_cols(z)

    aq = rotary(aq.reshape(bn, t_len, H_A * 2, HD_A), pos).reshape(bn, t_len, KV_A, G_A, 2, HD_A)
    ak = rotary(ak.reshape(bn, t_len, KV_A * 2, HD_A), pos).reshape(bn, t_len, KV_A, 2 * HD_A)
    av = av.reshape(bn, t_len, KV_A, VD_A)
    bq = rotary(bq.reshape(bn, t_len, H_B, HD_B), pos)
    bk = rotary(bk.reshape(bn, t_len, KV_B, HD_B), pos)
    bv = bv.reshape(bn, t_len, KV_B, HD_B)
    iq = rotary(iq.reshape(bn, t_len, H_I, D_I), pos)
    ik = rotary(ik.reshape(bn, t_len, 1, D_I), pos)[:, :, 0]
    iw = iw * (H_I ** -0.5)

    kv_a_new = jnp.stack([ak, av], axis=2)
    kv_b_new = jnp.stack([bk, bv], axis=2)
    kv_a = jnp.concatenate([pre_kv_a, kv_a_new], axis=1)
    s_len = kv_a.shape[1]
    ka = kv_a[:, :, 0].reshape(bn, s_len, KV_A, 2, HD_A)
    va = kv_a[:, :, 1]
    kv_b = jnp.concatenate([pre_kv_b, kv_b_new], axis=1)
    kb = kv_b[:, :, 0]
    vb = kv_b[:, :, 1]
    ki = jnp.concatenate([pre_idx, ik], axis=1)

    lam_init = 0.8 - 0.6 * math.exp(-0.3 * l)
    lam = (jnp.exp(jnp.sum(p['lam_q1'][l].astype(jnp.float32) * p['lam_k1'][l].astype(jnp.float32)))
           - jnp.exp(jnp.sum(p['lam_q2'][l].astype(jnp.float32) * p['lam_k2'][l].astype(jnp.float32)))
           + lam_init)

    fa = lambda q_, qp: diff_core(q_, qp, ka, va, lam)
    fb = lambda qi_, wi_, q_, qp: dsa_core(qi_, wi_, q_, qp, ki, kb, vb, k_sel)
    if blocked:
        ya = over_query_blocks(fa, (aq,), pos)
        yb = over_query_blocks(fb, (iq, iw, bq), pos)
    else:
        ya = fa(aq, pos)
        yb = fb(iq, iw, bq, pos)
    ya = rms_norm(ya, p['subln_a'][l]) * (1.0 - lam_init)
    ya = ya.reshape(bn, t_len, H_A * VD_A) @ p['w_out_a'][l]
    yb = yb @ p['w_out_b'][l]

    ca, cg = jnp.split(cin, 2, axis=-1)
    u = ca * jax.nn.sigmoid(cg)
    u_ext = jnp.concatenate([pre_conv_c, u], axis=1)
    yc = dwconv_causal(u_ext, p['conv_c_w'][l], p['conv_c_b'][l])
    yc = jax.nn.silu(layer_norm(yc, p['ln_c_g'][l], p['ln_c_b'][l])) @ p['w_out_c'][l]

    g = jax.nn.sigmoid(gz.reshape(bn, t_len, 3, D_MODEL))
    merged = g[:, :, 0] * ya + g[:, :, 1] * yb + g[:, :, 2] * yc
    x = x + merged @ p['w_o'][l]

    up = rms_norm(x, p['norm_ffn'][l]) @ p['w_up'][l]
    up_ext = jnp.concatenate([pre_conv_f, up], axis=1)
    hc = dwconv_causal(up_ext, p['conv_f_w'][l], p['conv_f_b'][l])
    hg, hv = jnp.split(hc, 2, axis=-1)
    x = x + (jax.nn.silu(hg) * hv) @ p['w_down'][l]

    new_state = (kv_a_new, kv_b_new, ik, u_ext[:, -(CONV_C - 1):], up_ext[:, -(CONV_F - 1):])
    return x, new_state


def setup_inputs(seed: int = 0) -> dict:
    key = jax.random.key(seed)
    ks = iter(jax.random.split(key, 40))
    f32 = jnp.float32

    def nrm(shape, scale):
        return scale * jax.random.normal(next(ks), shape, f32)

    n_pages = PAST_LEN // PAGE_SIZE
    n_used = DEC_BATCH * n_pages
    n_pool = n_used + max(1, n_used // 4)
    x_prompt = nrm((BATCH, SEQ, D_MODEL), 1.0)
    x_sample = nrm((DEC_BATCH, DEC_SEQ, D_MODEL), 1.0)
    cache_kv_a = nrm((DEPTH, n_pool, PAGE_SIZE, 2, KV_A, 2 * HD_A), 1.0)
    cache_kv_b = nrm((DEPTH, n_pool, PAGE_SIZE, 2, KV_B, HD_B), 1.0)
    cache_idx_b = nrm((DEPTH, n_pool, PAGE_SIZE, D_I), 1.0)
    state_conv_c = nrm((DEPTH, DEC_BATCH, CONV_C - 1, D_C), 1.0)
    state_conv_ffn = nrm((DEPTH, DEC_BATCH, CONV_F - 1, 2 * D_FF), 1.0)
    perm = jax.random.permutation(next(ks), n_pool)
    page_table = perm[:n_used].reshape(DEC_BATCH, n_pages).astype(jnp.int32)
    return {
        'x_prompt': x_prompt,
        'x_sample': x_sample,
        'cache_kv_a': cache_kv_a,
        'cache_kv_b': cache_kv_b,
        'cache_idx_b': cache_idx_b,
        'state_conv_c': state_conv_c,
        'state_conv_ffn': state_conv_ffn,
        'page_table': page_table,
        'meta_tokens': nrm((N_META, D_MODEL), 1.0),
        'norm_mix': 1.0 + nrm((DEPTH, D_MODEL), 0.02),
        'w_in': nrm((DEPTH, D_MODEL, N_IN), D_MODEL ** -0.5),
        'lam_q1': nrm((DEPTH, HD_A), 0.1),
        'lam_k1': nrm((DEPTH, HD_A), 0.1),
        'lam_q2': nrm((DEPTH, HD_A), 0.1),
        'lam_k2': nrm((DEPTH, HD_A), 0.1),
        'subln_a': 1.0 + nrm((DEPTH, VD_A), 0.02),
        'w_out_a': nrm((DEPTH, H_A * VD_A, D_MODEL), (H_A * VD_A) ** -0.5),
        'w_out_b': nrm((DEPTH, H_B * HD_B, D_MODEL), (H_B * HD_B) ** -0.5),
        'conv_c_w': nrm((DEPTH, CONV_C, D_C), CONV_C ** -0.5),
        'conv_c_b': nrm((DEPTH, D_C), 0.02),
        'ln_c_g': 1.0 + nrm((DEPTH, D_C), 0.02),
        'ln_c_b': nrm((DEPTH, D_C), 0.02),
        'w_out_c': nrm((DEPTH, D_C, D_MODEL), D_C ** -0.5),
        'w_o': nrm((DEPTH, D_MODEL, D_MODEL), D_MODEL ** -0.5),
        'norm_ffn': 1.0 + nrm((DEPTH, D_MODEL), 0.02),
        'w_up': nrm((DEPTH, D_MODEL, 2 * D_FF), D_MODEL ** -0.5),
        'conv_f_w': nrm((DEPTH, CONV_F, 2 * D_FF), CONV_F ** -0.5),
        'conv_f_b': nrm((DEPTH, 2 * D_FF), 0.02),
        'w_down': nrm((DEPTH, D_FF, D_MODEL), D_FF ** -0.5),
        'norm_final': 1.0 + nrm((D_MODEL,), 0.02),
    }


def reference(x_prompt, x_sample, cache_kv_a, cache_kv_b, cache_idx_b, state_conv_c, state_conv_ffn,
              page_table, meta_tokens, norm_mix, w_in, lam_q1, lam_k1, lam_q2, lam_k2, subln_a,
              w_out_a, w_out_b, conv_c_w, conv_c_b, ln_c_g, ln_c_b, w_out_c, w_o, norm_ffn, w_up,
              conv_f_w, conv_f_b, w_down, norm_final):
    p = {'norm_mix': norm_mix, 'w_in': w_in, 'lam_q1': lam_q1, 'lam_k1': lam_k1, 'lam_q2': lam_q2,
         'lam_k2': lam_k2, 'subln_a': subln_a, 'w_out_a': w_out_a, 'w_out_b': w_out_b,
         'conv_c_w': conv_c_w, 'conv_c_b': conv_c_b, 'ln_c_g': ln_c_g, 'ln_c_b': ln_c_b,
         'w_out_c': w_out_c, 'w_o': w_o, 'norm_ffn': norm_ffn, 'w_up': w_up,
         'conv_f_w': conv_f_w, 'conv_f_b': conv_f_b, 'w_down': w_down}
    dt = x_prompt.dtype
    n_prompt, seq = x_prompt.shape[:2]
    xp = jnp.concatenate([jnp.broadcast_to(meta_tokens[None].astype(dt), (n_prompt, N_META, D_MODEL)),
                          x_prompt], axis=1)
    pos_p = jnp.arange(xp.shape[1], dtype=jnp.int32)
    k_sel_p = min(TOPK_MAX, seq // 4)

    n_seq, n_new = x_sample.shape[:2]
    past_len = page_table.shape[1] * cache_kv_a.shape[2]
    pos_s = past_len + jnp.arange(n_new, dtype=jnp.int32)
    k_sel_s = min(TOPK_MAX, (past_len + n_new) // 4)
    xs = x_sample

    st_p = []
    st_s = []
    for l in range(DEPTH):
        pre_p = (jnp.zeros((n_prompt, 0, 2, KV_A, 2 * HD_A), dt),
                 jnp.zeros((n_prompt, 0, 2, KV_B, HD_B), dt),
                 jnp.zeros((n_prompt, 0, D_I), dt),
                 jnp.zeros((n_prompt, CONV_C - 1, D_C), dt),
                 jnp.zeros((n_prompt, CONV_F - 1, 2 * D_FF), dt))
        xp, sp = trunk_layer(xp, pos_p, l, p, pre_p, True, k_sel_p)
        st_p.append(sp)
        pre_s = (cache_kv_a[l, page_table].reshape(n_seq, past_len, 2, KV_A, 2 * HD_A),
                 cache_kv_b[l, page_table].reshape(n_seq, past_len, 2, KV_B, HD_B),
                 cache_idx_b[l, page_table].reshape(n_seq, past_len, D_I),
                 state_conv_c[l],
                 state_conv_ffn[l])
        xs, ss = trunk_layer(xs, pos_s, l, p, pre_s, False, k_sel_s)
        st_s.append(ss)

    def stk(states, i):
        return jnp.stack([s[i] for s in states], axis=0)

    y_prompt = rms_norm(xp, norm_final)[:, N_META:]
    y_sample = rms_norm(xs, norm_final)
    return (y_prompt, y_sample,
            stk(st_p, 0), stk(st_p, 1), stk(st_p, 2), stk(st_p, 3), stk(st_p, 4),
            stk(st_s, 0), stk(st_s, 1), stk(st_s, 2), stk(st_s, 3), stk(st_s, 4))
```

```python
import functools
import math

import numpy as np
import jax
import jax.numpy as jnp
from jax import lax
from jax.experimental import pallas as pl
from jax.experimental.pallas import tpu as pltpu

D_MODEL = 1024
N_META = 16
H_A, KV_A, HD_A = 4, 2, 64
G_A = H_A // KV_A
VD_A = 2 * HD_A
H_B, KV_B, HD_B = 8, 2, 64
G_B = H_B // KV_B
H_I, D_I = 8, 64
TOPK_MAX = 256
D_C, CONV_C = 512, 31
D_FF, CONV_F = 2816, 3
ROPE_THETA = 500000.0
ROT_HALF = 8
EPS = 1e-6
LN_EPS = 1e-5

LANE = 128
ROW_TILE = 256
Q_TILE = 128
KEY_TILE = 256
FF_CHUNK = 256
VMEM_LIMIT = 56 * 1024 * 1024

MASKED = -1e30
NO_SCORE = -3.0e38
INT_MIN = -(2 ** 31)

F32 = jnp.float32
BF16 = jnp.bfloat16

C_AQ, C_KVA, C_BQ, C_BKK, C_BVV, C_KVB, C_IQ, C_IKK, C_IW, C_CA, C_CG, C_GZ = (
    0, 512, 1024, 1536, 1792, 2048, 2304, 2816, 2944, 3072, 3584, 4096)
N_PROJ = C_GZ + 3 * D_MODEL


def _params(sem):
    return pltpu.CompilerParams(dimension_semantics=sem, vmem_limit_bytes=VMEM_LIMIT)


def _const_spec(shape):
    nd = len(shape)
    return pl.BlockSpec(shape, lambda *_: (0,) * nd)


def _rms(x, g):
    return x * lax.rsqrt(jnp.mean(x * x, axis=-1, keepdims=True) + EPS) * g


def _nt_dot(a, b):
    return lax.dot_general(a, b, (((1,), (1,)), ((), ())), preferred_element_type=F32)


def _lane_ids(shape):
    return lax.broadcasted_iota(jnp.int32, shape, len(shape) - 1)


def _in_proj_kernel(x_ref, g_ref, w_ref, cos_ref, sa_ref, sb_ref, valid_ref,
                    aq_ref, kva_ref, kvab_ref, bq_ref, bkk_ref, bvv_ref, kvb_ref,
                    iq_ref, ikk_ref, ikkb_ref, iw_ref, u_ref, gz_ref):
    xb = _rms(x_ref[...], g_ref[...]).astype(BF16)
    cos, sa, sb = cos_ref[...], sa_ref[...], sb_ref[...]

    def proj(start, width):
        return jnp.dot(xb, w_ref[:, start:start + width], preferred_element_type=F32)

    def rot(z):
        return z * cos + pltpu.roll(z, LANE - ROT_HALF, 1) * sa + pltpu.roll(z, ROT_HALF, 1) * sb

    def groups(z, n_rot):
        n = z.shape[1] // LANE
        return [rot(z[:, k * LANE:(k + 1) * LANE]) if k < n_rot else z[:, k * LANE:(k + 1) * LANE]
                for k in range(n)]

    def put(ref, k, val):
        ref[:, k * LANE:(k + 1) * LANE] = val.astype(ref.dtype)

    for k, zk in enumerate(groups(proj(C_AQ, 512), 4)):
        put(aq_ref, k, zk)
    for k, zk in enumerate(groups(proj(C_KVA, 512), 2)):
        put(kva_ref, k, zk)
        put(kvab_ref, k, zk)
    for k, zk in enumerate(groups(proj(C_BQ, 512), 4)):
        put(bq_ref, k, zk)
    for k, zk in enumerate(groups(proj(C_BKK, 256), 2)):
        put(bkk_ref, k, zk)
    for k, zk in enumerate(groups(proj(C_BVV, 256), 0)):
        put(bvv_ref, k, zk)
    for k, zk in enumerate(groups(proj(C_KVB, 256), 1)):
        put(kvb_ref, k, zk)
    for k, zk in enumerate(groups(proj(C_IQ, 512), 4)):
        put(iq_ref, k, zk)
    ikk = rot(proj(C_IKK, LANE))
    ikk_ref[...] = ikk
    ikkb_ref[...] = ikk.astype(BF16)
    iw_ref[...] = proj(C_IW, LANE) * (H_I ** -0.5)
    u_ref[...] = proj(C_CA, D_C) * jax.nn.sigmoid(proj(C_CG, D_C)) * valid_ref[...]
    gz_ref[...] = proj(C_GZ, 3 * D_MODEL)


def _in_proj(x, g, w, cos, sa, sb, valid, table_tiles):
    n = x.shape[0]
    nt = n // ROW_TILE
    row = lambda width: pl.BlockSpec((ROW_TILE, width), lambda i: (i, 0))
    tab = pl.BlockSpec((ROW_TILE, LANE), lambda i: (i % table_tiles, 0))
    outs = [(512, BF16), (512, F32), (512, BF16), (512, BF16), (256, BF16), (256, BF16), (256, F32),
            (512, BF16), (LANE, F32), (LANE, BF16), (LANE, F32), (D_C, F32), (3 * D_MODEL, F32)]
    return pl.pallas_call(
        _in_proj_kernel,
        grid=(nt,),
        in_specs=[row(D_MODEL), _const_spec((1, D_MODEL)), _const_spec((D_MODEL, N_PROJ)),
                  tab, tab, tab, pl.BlockSpec((ROW_TILE, 1), lambda i: (i % table_tiles, 0))],
        out_specs=[row(wd) for wd, _ in outs],
        out_shape=[jax.ShapeDtypeStruct((n, wd), dt) for wd, dt in outs],
        compiler_params=_params(("parallel",)),
        name="in_proj",
    )(x, g, w, cos, sa, sb, valid)


def _prep_w_in(w):
    o_aq, o_ak, o_av, o_bq, o_bk, o_bv, o_iq, o_iw, o_ik, o_cin, o_gz = (
        0, 512, 768, 1024, 1536, 1664, 1792, 2304, 2312, 2376, 3400)
    bk0, bk1 = w[:, o_bk:o_bk + 64], w[:, o_bk + 64:o_bk + 128]
    bv0, bv1 = w[:, o_bv:o_bv + 64], w[:, o_bv + 64:o_bv + 128]
    ik = w[:, o_ik:o_ik + D_I]
    cols = [w[:, o_aq:o_aq + 512], w[:, o_ak:o_ak + 256], w[:, o_av:o_av + 256], w[:, o_bq:o_bq + 512],
            bk0, bk0, bk1, bk1, bv0, bv0, bv1, bv1,
            w[:, o_bk:o_bk + 128], w[:, o_bv:o_bv + 128],
            w[:, o_iq:o_iq + 512], ik, ik,
            w[:, o_iw:o_iw + H_I], jnp.zeros((w.shape[0], LANE - H_I), w.dtype),
            w[:, o_cin:o_cin + 2 * D_C], w[:, o_gz:o_gz + 3 * D_MODEL]]
    out = jnp.concatenate(cols, axis=1).astype(BF16)
    assert out.shape[1] == N_PROJ
    return out


def _rotary_tables(pos):
    rot = 2 * ROT_HALF
    inv = jnp.power(ROPE_THETA, -jnp.arange(ROT_HALF, dtype=F32) * 2.0 / rot)
    ang = pos.astype(F32)[:, None] * inv[None, :]
    cos, sin = jnp.cos(ang), jnp.sin(ang)
    t = pos.shape[0]
    rest = 64 - rot
    one = jnp.ones((t, rest), F32)
    zero = jnp.zeros((t, rest), F32)
    zh = jnp.zeros((t, ROT_HALF), F32)
    c64 = jnp.concatenate([cos, cos, one], axis=1)
    sa64 = jnp.concatenate([-sin, zh, zero], axis=1)
    sb64 = jnp.concatenate([zh, sin, zero], axis=1)
    dup = lambda a: jnp.concatenate([a, a], axis=1)
    return dup(c64), dup(sa64), dup(sb64)


def _split_pair(pair):
    lo = _lane_ids(pair.shape) < 64
    zero = jnp.zeros_like(pair)
    return jnp.where(lo, pair, zero), jnp.where(lo, zero, pair)


def _monotone_key(score):
    b = lax.bitcast_convert_type(score, jnp.int32)
    return jnp.where(b < 0, b ^ jnp.int32(0x7FFFFFFF), b)


def _kth_largest_key(count_ge, k_sel, rows):
    def step(it, u):
        bit = lax.shift_left(jnp.int32(1), jnp.int32(31) - it)
        cand = u | bit
        cnt = count_ge(cand ^ jnp.int32(INT_MIN))
        return jnp.where(cnt >= k_sel, cand, u)

    u = lax.fori_loop(0, 32, step, jnp.zeros((rows, 1), jnp.int32))
    return u ^ jnp.int32(INT_MIN)


def _attn_a_kernel(lam_ref, q_ref, kv_ref, sub_ref, o_ref, *, pad, out_scale):
    i = pl.program_id(1)
    q0 = i * Q_TILE

    @pl.when(q0 + Q_TILE <= pad)
    def _():
        o_ref[...] = jnp.zeros_like(o_ref)

    @pl.when(q0 + Q_TILE > pad)
    def _():
        lam = lam_ref[0]
        n_kb = (q0 + Q_TILE + KEY_TILE - 1) // KEY_TILE
        r = lax.broadcasted_iota(jnp.int32, (Q_TILE, 1), 0) + q0
        rowpos = jnp.concatenate([r] * 4, axis=0)
        for kv in range(KV_A):
            parts = []
            for g in range(G_A):
                c0 = (kv * G_A + g) * LANE
                parts += list(_split_pair(q_ref[:, c0:c0 + LANE]))
            qs = jnp.concatenate(parts, axis=0)

            def body(j, carry, kv=kv, qs=qs):
                m, l, acc = carry
                k0 = pl.multiple_of(j * KEY_TILE, KEY_TILE)
                k = kv_ref[pl.ds(k0, KEY_TILE), kv * LANE:(kv + 1) * LANE]
                v = kv_ref[pl.ds(k0, KEY_TILE), (KV_A + kv) * LANE:(KV_A + kv + 1) * LANE]
                s = _nt_dot(qs, k) * (HD_A ** -0.5)
                col = _lane_ids(s.shape) + k0
                s = jnp.where((col <= rowpos) & (col >= pad), s, MASKED)
                m_new = jnp.maximum(m, jnp.max(s, axis=1, keepdims=True))
                alpha = jnp.exp(m - m_new)
                p = jnp.exp(s - m_new)
                l = alpha * l + jnp.sum(p, axis=1, keepdims=True)
                acc = alpha * acc + jnp.dot(p.astype(BF16), v, preferred_element_type=F32)
                return m_new, l, acc

            rows = 4 * Q_TILE
            m, l, acc = lax.fori_loop(
                0, n_kb, body,
                (jnp.full((rows, 1), MASKED, F32), jnp.zeros((rows, 1), F32), jnp.zeros((rows, LANE), F32)))
            o = acc / l
            for g in range(G_A):
                b0 = g * 2 * Q_TILE
                y = o[b0:b0 + Q_TILE] - lam * o[b0 + Q_TILE:b0 + 2 * Q_TILE]
                y = _rms(y, sub_ref[...]) * out_scale
                c0 = (kv * G_A + g) * LANE
                o_ref[:, c0:c0 + LANE] = y.astype(o_ref.dtype)


def _attn_a_prompt(lam, aq, kvab, subln, pad, out_scale):
    b, tp, _ = aq.shape
    return pl.pallas_call(
        functools.partial(_attn_a_kernel, pad=pad, out_scale=out_scale),
        grid=(b, tp // Q_TILE),
        in_specs=[pl.BlockSpec(memory_space=pltpu.SMEM),
                  pl.BlockSpec((None, Q_TILE, 512), lambda bi, i: (bi, i, 0)),
                  pl.BlockSpec((None, tp, 512), lambda bi, i: (bi, 0, 0)),
                  _const_spec((1, LANE))],
        out_specs=pl.BlockSpec((None, Q_TILE, 512), lambda bi, i: (bi, i, 0)),
        out_shape=jax.ShapeDtypeStruct((b, tp, 512), BF16),
        compiler_params=_params(("parallel", "parallel")),
        name="attn_a_prompt",
    )(lam, aq, kvab, subln)


def _attn_b_kernel(iq_ref, iw_ref, q_ref, ik_ref, kk_ref, vv_ref, o_ref, key_ref, *, pad, k_sel):
    i = pl.program_id(1)
    q0 = i * Q_TILE

    @pl.when(q0 + Q_TILE <= pad)
    def _():
        o_ref[...] = jnp.zeros_like(o_ref)

    @pl.when(q0 + Q_TILE > pad)
    def _():
        n_kb = (q0 + Q_TILE + KEY_TILE - 1) // KEY_TILE
        rowpos = lax.broadcasted_iota(jnp.int32, (Q_TILE, 1), 0) + q0
        no_key = _monotone_key(jnp.full((1, 1), NO_SCORE, F32))

        parts = []
        for pr in range(H_I // 2):
            parts += list(_split_pair(iq_ref[:, pr * LANE:(pr + 1) * LANE]))
        qi = jnp.concatenate(parts, axis=0)
        iw = iw_ref[...]

        def score_body(j, _):
            k0 = pl.multiple_of(j * KEY_TILE, KEY_TILE)
            d = jnp.maximum(_nt_dot(qi, ik_ref[pl.ds(k0, KEY_TILE), :]) * (D_I ** -0.5), 0.0)
            sc = jnp.zeros((Q_TILE, KEY_TILE), F32)
            for h in range(H_I):
                sc = sc + iw[:, h:h + 1] * d[h * Q_TILE:(h + 1) * Q_TILE]
            col = _lane_ids(sc.shape) + k0
            sc = jnp.where((col <= rowpos) & (col >= pad), sc, NO_SCORE)
            key_ref[j] = _monotone_key(sc)
            return 0

        lax.fori_loop(0, n_kb, score_body, 0)

        def count_ge(t):
            def cbody(j, c):
                return c + jnp.where(key_ref[j] >= t, 1.0, 0.0)
            c = lax.fori_loop(0, n_kb, cbody, jnp.zeros((Q_TILE, KEY_TILE), F32))
            return jnp.sum(c, axis=1, keepdims=True)

        thr = jnp.maximum(_kth_largest_key(count_ge, k_sel, Q_TILE), no_key + 1)

        for kv in range(KV_B):
            parts = []
            for pr in range(G_B // 2):
                c0 = (kv * (G_B // 2) + pr) * LANE
                parts += list(_split_pair(q_ref[:, c0:c0 + LANE]))
            qs = jnp.concatenate(parts, axis=0)

            def body(j, carry, kv=kv, qs=qs):
                m, l, acc = carry
                k0 = pl.multiple_of(j * KEY_TILE, KEY_TILE)
                k = kk_ref[pl.ds(k0, KEY_TILE), kv * LANE:(kv + 1) * LANE]
                v = vv_ref[pl.ds(k0, KEY_TILE), kv * LANE:(kv + 1) * LANE]
                sel = key_ref[j] >= thr
                sel = jnp.concatenate([sel] * G_B, axis=0)
                s = jnp.where(sel, _nt_dot(qs, k) * (HD_B ** -0.5), MASKED)
                m_new = jnp.maximum(m, jnp.max(s, axis=1, keepdims=True))
                alpha = jnp.exp(m - m_new)
                p = jnp.where(sel, jnp.exp(s - m_new), 0.0)
                l = alpha * l + jnp.sum(p, axis=1, keepdims=True)
                acc = alpha * acc + jnp.dot(p.astype(BF16), v, preferred_element_type=F32)
                return m_new, l, acc

            rows = G_B * Q_TILE
            m, l, acc = lax.fori_loop(
                0, n_kb, body,
                (jnp.full((rows, 1), MASKED, F32), jnp.zeros((rows, 1), F32), jnp.zeros((rows, LANE), F32)))
            o = acc / jnp.where(l > 0.0, l, 1.0)
            lo = _lane_ids((Q_TILE, LANE)) < 64
            for pr in range(G_B // 2):
                b0 = pr * 2 * Q_TILE
                pair = jnp.where(lo, o[b0:b0 + Q_TILE], o[b0 + Q_TILE:b0 + 2 * Q_TILE])
                c0 = (kv * (G_B // 2) + pr) * LANE
                o_ref[:, c0:c0 + LANE] = pair.astype(o_ref.dtype)


def _attn_b_prompt(iq, iw, bq, ikkb, bkk, bvv, pad, k_sel):
    b, tp, _ = bq.shape
    qspec = lambda width: pl.BlockSpec((None, Q_TILE, width), lambda bi, i: (bi, i, 0))
    full = lambda width: pl.BlockSpec((None, tp, width), lambda bi, i: (bi, 0, 0))
    return pl.pallas_call(
        functools.partial(_attn_b_kernel, pad=pad, k_sel=k_sel),
        grid=(b, tp // Q_TILE),
        in_specs=[qspec(512), qspec(LANE), qspec(512), full(LANE), full(256), full(256)],
        out_specs=qspec(512),
        out_shape=jax.ShapeDtypeStruct((b, tp, 512), BF16),
        scratch_shapes=[pltpu.VMEM((tp // KEY_TILE, Q_TILE, KEY_TILE), jnp.int32)],
        compiler_params=_params(("parallel", "parallel")),
        name="attn_b_prompt",
    )(iq, iw, bq, ikkb, bkk, bvv)


def _ln_swish(y, g, b):
    mu = jnp.mean(y, axis=-1, keepdims=True)
    yc = y - mu
    z = yc * lax.rsqrt(jnp.mean(yc * yc, axis=-1, keepdims=True) + LN_EPS) * g + b
    return z * jax.nn.sigmoid(z)


HALO = 32


def _conv_c_prompt_kernel(prev_ref, cur_ref, w_ref, b_ref, g_ref, beta_ref, o_ref, ext_ref):
    ext_ref[0:HALO, :] = prev_ref[ROW_TILE - HALO:ROW_TILE, :]
    ext_ref[HALO:HALO + ROW_TILE, :] = cur_ref[...]
    acc = jnp.zeros((ROW_TILE, D_C), F32) + b_ref[...]
    base = HALO - (CONV_C - 1)
    for j in range(CONV_C):
        acc = acc + w_ref[j:j + 1, :] * ext_ref[base + j:base + j + ROW_TILE, :]
    o_ref[...] = _ln_swish(acc, g_ref[...], beta_ref[...]).astype(o_ref.dtype)


def _conv_c_prompt(u, w, b, g, beta):
    n = u.shape[0]
    return pl.pallas_call(
        _conv_c_prompt_kernel,
        grid=(n // ROW_TILE,),
        in_specs=[pl.BlockSpec((ROW_TILE, D_C), lambda i: (jnp.maximum(i - 1, 0), 0)),
                  pl.BlockSpec((ROW_TILE, D_C), lambda i: (i, 0)),
                  _const_spec((CONV_C, D_C)), _const_spec((1, D_C)), _const_spec((1, D_C)),
                  _const_spec((1, D_C))],
        out_specs=pl.BlockSpec((ROW_TILE, D_C), lambda i: (i, 0)),
        out_shape=jax.ShapeDtypeStruct((n, D_C), BF16),
        scratch_shapes=[pltpu.VMEM((HALO + ROW_TILE, D_C), F32)],
        compiler_params=_params(("parallel",)),
        name="conv_c_prompt",
    )(u, u, w, b, g, beta)


def _conv_c_sample_kernel(st_ref, u_ref, w_ref, b_ref, g_ref, beta_ref, o_ref):
    n_hist = st_ref.shape[0]
    for t in range(u_ref.shape[0]):
        acc = jnp.zeros(o_ref.shape[1:], F32) + b_ref[...]
        for j in range(CONV_C):
            e = t + j
            row = st_ref[e] if e < n_hist else u_ref[e - n_hist]
            acc = acc + w_ref[j:j + 1, :] * row
        o_ref[t] = _ln_swish(acc, g_ref[...], beta_ref[...]).astype(o_ref.dtype)


def _conv_c_sample(st_t, u_t, w, b, g, beta):
    n_new, n_seq, _ = u_t.shape
    return pl.pallas_call(
        _conv_c_sample_kernel,
        grid=(1,),
        in_specs=[_const_spec(st_t.shape), _const_spec(u_t.shape), _const_spec((CONV_C, D_C)),
                  _const_spec((1, D_C)), _const_spec((1, D_C)), _const_spec((1, D_C))],
        out_specs=_const_spec((n_new, n_seq, D_C)),
        out_shape=jax.ShapeDtypeStruct((n_new, n_seq, D_C), BF16),
        compiler_params=_params(("arbitrary",)),
        name="conv_c_sample",
    )(st_t, u_t, w, b, g, beta)


def _merge_kernel(ya_ref, yb_ref, yc_ref, gz_ref, x_ref, wa_ref, wb_ref, wc_ref, wo_ref, o_ref):
    merged = jnp.zeros((ROW_TILE, D_MODEL), F32)
    for k, (y_ref, w_ref) in enumerate(((ya_ref, wa_ref), (yb_ref, wb_ref), (yc_ref, wc_ref))):
        gate = jax.nn.sigmoid(gz_ref[:, k * D_MODEL:(k + 1) * D_MODEL])
        merged = merged + gate * jnp.dot(y_ref[...], w_ref[...], preferred_element_type=F32)
    o_ref[...] = x_ref[...] + jnp.dot(merged.astype(BF16), wo_ref[...], preferred_element_type=F32)


def _merge(ya, yb, yc, gz, x, wa, wb, wc, wo):
    n = x.shape[0]
    row = lambda width: pl.BlockSpec((ROW_TILE, width), lambda i: (i, 0))
    return pl.pallas_call(
        _merge_kernel,
        grid=(n // ROW_TILE,),
        in_specs=[row(512), row(512), row(D_C), row(3 * D_MODEL), row(D_MODEL),
                  _const_spec(wa.shape), _const_spec(wb.shape), _const_spec(wc.shape), _const_spec(wo.shape)],
        out_specs=row(D_MODEL),
        out_shape=jax.ShapeDtypeStruct((n, D_MODEL), F32),
        compiler_params=_params(("parallel",)),
        name="merge",
    )(ya, yb, yc, gz, x, wa, wb, wc, wo)


TAIL = 8


def _ffn_kernel(*refs, carried, final_norm):
    if carried:
        (x_ref, g_ref, wu_ref, cw_ref, cb_ref, wd_ref, gf_ref, valid_ref,
         o_ref, tail_ref, carry_ref) = refs
    else:
        (x_ref, g_ref, wu_ref, cw_ref, cb_ref, wd_ref, gf_ref, h1_ref, h2_ref,
         o_ref, up_ref) = refs
    x = x_ref[...]
    xb = _rms(x, g_ref[...]).astype(BF16)
    row = lax.broadcasted_iota(jnp.int32, (ROW_TILE, 1), 0)

    if carried:
        @pl.when(pl.program_id(0) == 0)
        def _():
            carry_ref[...] = jnp.zeros_like(carry_ref)
        valid = valid_ref[...]
    else:
        t_in_seq = row % 4

    def conv(c0):
        up = jnp.dot(xb, wu_ref[:, c0:c0 + FF_CHUNK], preferred_element_type=F32)
        if carried:
            up = up * valid
            old = carry_ref[:, c0:c0 + FF_CHUNK]
            p1 = jnp.where(row == 0, old[TAIL - 1:TAIL], pltpu.roll(up, 1, 0))
            p2 = jnp.where(row == 0, old[TAIL - 2:TAIL - 1],
                           jnp.where(row == 1, old[TAIL - 1:TAIL], pltpu.roll(up, 2, 0)))
            carry_ref[:, c0:c0 + FF_CHUNK] = up[ROW_TILE - TAIL:]
            tail_ref[:, c0:c0 + FF_CHUNK] = up[ROW_TILE - TAIL:]
        else:
            p1 = jnp.where(t_in_seq < 1, h1_ref[:, c0:c0 + FF_CHUNK], pltpu.roll(up, 1, 0))
            p2 = jnp.where(t_in_seq < 2, h2_ref[:, c0:c0 + FF_CHUNK], pltpu.roll(up, 2, 0))
            up_ref[:, c0:c0 + FF_CHUNK] = up
        w = cw_ref[:, c0:c0 + FF_CHUNK]
        return w[0:1] * p2 + w[1:2] * p1 + w[2:3] * up + cb_ref[:, c0:c0 + FF_CHUNK]

    acc = jnp.zeros((ROW_TILE, D_MODEL), F32)
    for c in range(D_FF // FF_CHUNK):
        hg = conv(c * FF_CHUNK)
        hv = conv(D_FF + c * FF_CHUNK)
        h = (hg * jax.nn.sigmoid(hg) * hv).astype(BF16)
        acc = acc + jnp.dot(h, wd_ref[c * FF_CHUNK:(c + 1) * FF_CHUNK, :], preferred_element_type=F32)
    y = x + acc
    o_ref[...] = _rms(y, gf_ref[...]) if final_norm else y


def _ffn(x, g, wu, cw, cb, wd, gf, extra, carried, final_norm):
    n = x.shape[0]
    nt = n // ROW_TILE
    row = lambda width: pl.BlockSpec((ROW_TILE, width), lambda i: (i, 0))
    in_specs = [row(D_MODEL), _const_spec((1, D_MODEL)), _const_spec(wu.shape), _const_spec(cw.shape),
                _const_spec(cb.shape), _const_spec(wd.shape), _const_spec((1, D_MODEL))]
    if carried:
        in_specs += [pl.BlockSpec((ROW_TILE, 1), lambda i: (i, 0))]
        out_specs = [row(D_MODEL), pl.BlockSpec((None, TAIL, 2 * D_FF), lambda i: (i, 0, 0))]
        out_shape = [jax.ShapeDtypeStruct((n, D_MODEL), F32), jax.ShapeDtypeStruct((nt, TAIL, 2 * D_FF), F32)]
        scratch = [pltpu.VMEM((TAIL, 2 * D_FF), F32)]
    else:
        in_specs += [row(2 * D_FF), row(2 * D_FF)]
        out_specs = [row(D_MODEL), row(2 * D_FF)]
        out_shape = [jax.ShapeDtypeStruct((n, D_MODEL), F32), jax.ShapeDtypeStruct((n, 2 * D_FF), F32)]
        scratch = []
    return pl.pallas_call(
        functools.partial(_ffn_kernel, carried=carried, final_norm=final_norm),
        grid=(nt,),
        in_specs=in_specs, out_specs=out_specs, out_shape=out_shape, scratch_shapes=scratch,
        compiler_params=_params(("arbitrary",)),
        name="ffn_prompt" if carried else "ffn_sample",
    )(x, g, wu, cw, cb, wd, gf, *extra)


S_ROWS = 8


def _new_key_mask(shape, n_past, n_new):
    col = _lane_ids(shape)
    t = lax.broadcasted_iota(jnp.int32, shape, 0) % S_ROWS
    return (col < n_past) | ((col - n_past <= t) & (col < n_past + n_new))


def _pad_rows(a, rows):
    return jnp.concatenate([a, jnp.zeros((rows - a.shape[0], a.shape[1]), a.dtype)], axis=0)


def _attn_a_sample_kernel(pt_ref, lam_ref, q_ref, new_ref, sub_ref, *rest, n_pages, page, n_new, out_scale):
    page_refs, o_ref = rest[:n_pages], rest[n_pages]
    lam = lam_ref[0]
    n_past = n_pages * page
    q = q_ref[...].astype(F32)
    new = _pad_rows(new_ref[...], LANE)
    for kv in range(KV_A):
        parts = []
        for g in range(G_A):
            c0 = (kv * G_A + g) * LANE
            parts += list(_split_pair(q[:, c0:c0 + LANE]))
        qs = jnp.concatenate(parts, axis=0).astype(BF16)
        ksl = slice(kv * LANE, (kv + 1) * LANE)
        vsl = slice((KV_A + kv) * LANE, (KV_A + kv + 1) * LANE)
        k = jnp.concatenate([r[:, ksl].astype(BF16) for r in page_refs] + [new[:, ksl].astype(BF16)], axis=0)
        v = jnp.concatenate([r[:, vsl].astype(BF16) for r in page_refs] + [new[:, vsl].astype(BF16)], axis=0)
        s = _nt_dot(qs, k) * (HD_A ** -0.5)
        s = jnp.where(_new_key_mask(s.shape, n_past, n_new), s, MASKED)
        p = jnp.exp(s - jnp.max(s, axis=1, keepdims=True))
        o = jnp.dot(p.astype(BF16), v, preferred_element_type=F32) / jnp.sum(p, axis=1, keepdims=True)
        for g in range(G_A):
            b0 = g * 2 * S_ROWS
            y = o[b0:b0 + S_ROWS] - lam * o[b0 + S_ROWS:b0 + 2 * S_ROWS]
            y = _rms(y, sub_ref[...]) * out_scale
            c0 = (kv * G_A + g) * LANE
            o_ref[:, c0:c0 + LANE] = y.astype(o_ref.dtype)


def _page_specs(layer, n_pages, page, width):
    return [pl.BlockSpec((None, None, page, width),
                         lambda s, pt, j=j: (layer, pt[s * n_pages + j], 0, 0)) for j in range(n_pages)]


def _attn_a_sample(pt, lam, aq, new, subln, cache, layer, n_new, out_scale):
    n_seq = aq.shape[0]
    n_pages = pt.shape[0] // n_seq
    page = cache.shape[2]
    seq = lambda width: pl.BlockSpec((None, S_ROWS, width), lambda s, pt: (s, 0, 0))
    grid_spec = pltpu.PrefetchScalarGridSpec(
        num_scalar_prefetch=1,
        grid=(n_seq,),
        in_specs=[pl.BlockSpec(memory_space=pltpu.SMEM), seq(512), seq(512),
                  pl.BlockSpec((1, LANE), lambda s, pt: (0, 0))] + _page_specs(layer, n_pages, page, 512),
        out_specs=seq(512),
    )
    return pl.pallas_call(
        functools.partial(_attn_a_sample_kernel, n_pages=n_pages, page=page, n_new=n_new, out_scale=out_scale),
        grid_spec=grid_spec,
        out_shape=jax.ShapeDtypeStruct((n_seq, S_ROWS, 512), BF16),
        compiler_params=_params(("parallel",)),
        name="attn_a_sample",
    )(pt, lam, aq, new, subln, *([cache] * n_pages))


def _attn_b_sample_kernel(pt_ref, iq_ref, iw_ref, q_ref, newkv_ref, newik_ref, *rest,
                          n_pages, page, n_new, k_sel):
    kv_refs, ik_refs, o_ref = rest[:n_pages], rest[n_pages:2 * n_pages], rest[2 * n_pages]
    n_past = n_pages * page
    lo = _lane_ids((S_ROWS, LANE)) < 64
    zero = jnp.zeros((S_ROWS, LANE), F32)

    iq = iq_ref[...].astype(F32)
    parts = []
    for pr in range(H_I // 2):
        pair = iq[:, pr * LANE:(pr + 1) * LANE]
        parts += [pair[:, :D_I], pltpu.roll(pair, 64, 1)[:, :D_I]]
    qi = jnp.concatenate(parts, axis=0).astype(BF16)
    ik = jnp.concatenate([r[...].astype(BF16) for r in ik_refs]
                         + [_pad_rows(newik_ref[:, :D_I], LANE).astype(BF16)], axis=0)
    d = jnp.maximum(_nt_dot(qi, ik) * (D_I ** -0.5), 0.0)
    iw = iw_ref[...]
    sc = jnp.zeros((S_ROWS, d.shape[1]), F32)
    for h in range(H_I):
        sc = sc + iw[:, h:h + 1] * d[h * S_ROWS:(h + 1) * S_ROWS]
    sc = jnp.where(_new_key_mask(sc.shape, n_past, n_new), sc, NO_SCORE)
    keys = _monotone_key(sc)
    no_key = _monotone_key(jnp.full((1, 1), NO_SCORE, F32))

    def count_ge(t):
        return jnp.sum(jnp.where(keys >= t, 1.0, 0.0), axis=1, keepdims=True)

    thr = jnp.maximum(_kth_largest_key(count_ge, k_sel, S_ROWS), no_key + 1)
    sel = jnp.concatenate([keys >= thr] * G_B, axis=0)

    q = q_ref[...].astype(F32)
    newkv = _pad_rows(newkv_ref[...], LANE)
    k = jnp.concatenate([r[:, 0:LANE].astype(BF16) for r in kv_refs] + [newkv[:, 0:LANE].astype(BF16)], axis=0)
    v = jnp.concatenate([r[:, LANE:2 * LANE].astype(BF16) for r in kv_refs]
                        + [newkv[:, LANE:2 * LANE].astype(BF16)], axis=0)
    for kv in range(KV_B):
        parts = []
        for pr in range(G_B // 2):
            c0 = (kv * (G_B // 2) + pr) * LANE
            pair = q[:, c0:c0 + LANE]
            swapped = pltpu.roll(pair, 64, 1)
            if kv == 0:
                parts += [jnp.where(lo, pair, zero), jnp.where(lo, swapped, zero)]
            else:
                parts += [jnp.where(lo, zero, swapped), jnp.where(lo, zero, pair)]
        qs = jnp.concatenate(parts, axis=0).astype(BF16)
        s = jnp.where(sel, _nt_dot(qs, k) * (HD_B ** -0.5), MASKED)
        p = jnp.where(sel, jnp.exp(s - jnp.max(s, axis=1, keepdims=True)), 0.0)
        l = jnp.sum(p, axis=1, keepdims=True)
        o = jnp.dot(p.astype(BF16), v, preferred_element_type=F32) / jnp.where(l > 0.0, l, 1.0)
        for pr in range(G_B // 2):
            even = o[2 * pr * S_ROWS:(2 * pr + 1) * S_ROWS]
            odd = o[(2 * pr + 1) * S_ROWS:(2 * pr + 2) * S_ROWS]
            if kv == 0:
                pair = jnp.where(lo, even, pltpu.roll(odd, 64, 1))
            else:
                pair = jnp.where(lo, pltpu.roll(even, 64, 1), odd)
            c0 = (kv * (G_B // 2) + pr) * LANE
            o_ref[:, c0:c0 + LANE] = pair.astype(o_ref.dtype)


def _attn_b_sample(pt, iq, iw, bq, newkv, newik, cache_kv, cache_ik, layer, n_new, k_sel):
    n_seq = bq.shape[0]
    n_pages = pt.shape[0] // n_seq
    page = cache_kv.shape[2]
    seq = lambda width: pl.BlockSpec((None, S_ROWS, width), lambda s, pt: (s, 0, 0))
    grid_spec = pltpu.PrefetchScalarGridSpec(
        num_scalar_prefetch=1,
        grid=(n_seq,),
        in_specs=[seq(512), seq(LANE), seq(512), seq(256), seq(LANE)]
        + _page_specs(layer, n_pages, page, 256) + _page_specs(layer, n_pages, page, D_I),
        out_specs=seq(512),
    )
    return pl.pallas_call(
        functools.partial(_attn_b_sample_kernel, n_pages=n_pages, page=page, n_new=n_new, k_sel=k_sel),
        grid_spec=grid_spec,
        out_shape=jax.ShapeDtypeStruct((n_seq, S_ROWS, 512), BF16),
        compiler_params=_params(("parallel",)),
        name="attn_b_sample",
    )(pt, iq, iw, bq, newkv, newik, *([cache_kv] * n_pages), *([cache_ik] * n_pages))


def kernel(x_prompt, x_sample, cache_kv_a, cache_kv_b, cache_idx_b, state_conv_c, state_conv_ffn, page_table, meta_tokens, norm_mix, w_in, lam_q1, lam_k1, lam_q2, lam_k2, subln_a, w_out_a, w_out_b, conv_c_w, conv_c_b, ln_c_g, ln_c_b, w_out_c, w_o, norm_ffn, w_up, conv_f_w, conv_f_b, w_down, norm_final):
    depth = w_in.shape[0]
    n_prompt, seq, _ = x_prompt.shape
    t_len = seq + N_META
    pad = (-t_len) % ROW_TILE
    tp = t_len + pad
    n_seq, n_new, _ = x_sample.shape
    n_pool, page = cache_kv_a.shape[1:3]
    n_pages = page_table.shape[1]
    past_len = n_pages * page
    k_sel_p = min(TOPK_MAX, seq // 4)
    k_sel_s = min(TOPK_MAX, (past_len + n_new) // 4)
    assert n_new <= S_ROWS and n_new == 4 and (n_seq * n_new) % ROW_TILE == 0
    assert tp % ROW_TILE == 0 and k_sel_p <= KEY_TILE

    xp = jnp.concatenate([jnp.zeros((n_prompt, pad, D_MODEL), F32),
                          jnp.broadcast_to(meta_tokens[None], (n_prompt, N_META, D_MODEL)), x_prompt], axis=1)
    xp = xp.reshape(n_prompt * tp, D_MODEL)
    xs = x_sample.reshape(n_seq * n_new, D_MODEL)
    tab_p = _rotary_tables(jnp.arange(tp, dtype=jnp.int32) - pad)
    tab_s = _rotary_tables(past_len + (jnp.arange(ROW_TILE, dtype=jnp.int32) % n_new))
    valid_p = (jnp.arange(tp) >= pad).astype(F32)[:, None]
    valid_s = jnp.ones((ROW_TILE, 1), F32)
    valid_p_flat = jnp.tile(valid_p, (n_prompt, 1))
    pt_flat = page_table.reshape(-1).astype(jnp.int32)
    cache_a = cache_kv_a.reshape(depth, n_pool, page, 2 * KV_A * 2 * HD_A)
    cache_b = cache_kv_b.reshape(depth, n_pool, page, 2 * KV_B * HD_B)

    row1 = lambda a: a.reshape(1, -1)
    pad_seq = lambda a: jnp.pad(a.reshape(n_seq, n_new, -1), ((0, 0), (0, S_ROWS - n_new), (0, 0)))
    unpad_seq = lambda a: a[:, :n_new].reshape(n_seq * n_new, -1)

    outs_p = [[] for _ in range(5)]
    outs_s = [[] for _ in range(5)]
    for l in range(depth):
        lam_init = 0.8 - 0.6 * math.exp(-0.3 * l)
        lam = (jnp.exp(jnp.sum(lam_q1[l] * lam_k1[l])) - jnp.exp(jnp.sum(lam_q2[l] * lam_k2[l]))
               + lam_init).reshape(1).astype(F32)
        out_scale = 1.0 - lam_init
        last = l == depth - 1
        w_in_l = _prep_w_in(w_in[l])
        wa, wb, wc, wo = (w_out_a[l].astype(BF16), w_out_b[l].astype(BF16), w_out_c[l].astype(BF16),
                          w_o[l].astype(BF16))
        wu, wd = w_up[l].astype(BF16), w_down[l].astype(BF16)
        ffn_w = (row1(norm_ffn[l]), wu, conv_f_w[l], row1(conv_f_b[l]), wd, row1(norm_final))
        conv_w = (conv_c_w[l], row1(conv_c_b[l]), row1(ln_c_g[l]), row1(ln_c_b[l]))

        (aq, kva, kvab, bq, bkk, bvv, kvb, iq, ikk, ikkb, iw, u, gz) = _in_proj(
            xp, row1(norm_mix[l]), w_in_l, *tab_p, valid_p, tp // ROW_TILE)
        per_seq = lambda a: a.reshape(n_prompt, tp, a.shape[-1])
        ya = _attn_a_prompt(lam, per_seq(aq), per_seq(kvab), row1(subln_a[l]), pad, out_scale)
        yb = _attn_b_prompt(per_seq(iq), per_seq(iw), per_seq(bq), per_seq(ikkb), per_seq(bkk), per_seq(bvv),
                            pad, k_sel_p)
        yc = _conv_c_prompt(u, *conv_w)
        xp = _merge(ya.reshape(-1, 512), yb.reshape(-1, 512), yc, gz, xp, wa, wb, wc, wo)
        xp, tails = _ffn(xp, *ffn_w, (valid_p_flat,), carried=True, final_norm=last)
        outs_p[0].append(per_seq(kva)[:, pad:].reshape(n_prompt, t_len, 2, KV_A, 2 * HD_A))
        outs_p[1].append(per_seq(kvb)[:, pad:].reshape(n_prompt, t_len, 2, KV_B, HD_B))
        outs_p[2].append(per_seq(ikk)[:, pad:, :D_I])
        outs_p[3].append(per_seq(u)[:, tp - (CONV_C - 1):])
        tails = tails.reshape(n_prompt, tp // ROW_TILE, TAIL, 2 * D_FF)
        outs_p[4].append(tails[:, -1, TAIL - (CONV_F - 1):])

        (aq, kva, kvab, bq, bkk, bvv, kvb, iq, ikk, ikkb, iw, u, gz) = _in_proj(
            xs, row1(norm_mix[l]), w_in_l, *tab_s, valid_s, 1)
        ya = _attn_a_sample(pt_flat, lam, pad_seq(aq), pad_seq(kva), row1(subln_a[l]), cache_a, l, n_new,
                            out_scale)
        yb = _attn_b_sample(pt_flat, pad_seq(iq), pad_seq(iw), pad_seq(bq), pad_seq(kvb), pad_seq(ikk),
                            cache_b, cache_idx_b, l, n_new, k_sel_s)
        u_seq = u.reshape(n_seq, n_new, D_C)
        yc = _conv_c_sample(jnp.swapaxes(state_conv_c[l], 0, 1), jnp.swapaxes(u_seq, 0, 1), *conv_w)
        yc = jnp.swapaxes(yc, 0, 1).reshape(n_seq * n_new, D_C)
        xs = _merge(unpad_seq(ya), unpad_seq(yb), yc, gz, xs, wa, wb, wc, wo)
        st_f = state_conv_ffn[l]
        zrow = jnp.zeros_like(st_f[:, :1])
        h1 = jnp.concatenate([st_f[:, 1:2], zrow, zrow, zrow], axis=1).reshape(n_seq * n_new, 2 * D_FF)
        h2 = jnp.concatenate([st_f[:, 0:1], st_f[:, 1:2], zrow, zrow], axis=1).reshape(n_seq * n_new, 2 * D_FF)
        xs, up = _ffn(xs, *ffn_w, (h1, h2), carried=False, final_norm=last)
        outs_s[0].append(kva.reshape(n_seq, n_new, 2, KV_A, 2 * HD_A))
        outs_s[1].append(kvb.reshape(n_seq, n_new, 2, KV_B, HD_B))
        outs_s[2].append(ikk[:, :D_I].reshape(n_seq, n_new, D_I))
        outs_s[3].append(jnp.concatenate([state_conv_c[l], u_seq], axis=1)[:, -(CONV_C - 1):])
        up_ext = jnp.concatenate([st_f, up.reshape(n_seq, n_new, 2 * D_FF)], axis=1)
        outs_s[4].append(up_ext[:, -(CONV_F - 1):])

    y_prompt = xp.reshape(n_prompt, tp, D_MODEL)[:, pad + N_META:]
    y_sample = xs.reshape(n_seq, n_new, D_MODEL)
    stk = lambda xs_: jnp.stack(xs_, axis=0)
    return (y_prompt, y_sample, *[stk(o) for o in outs_p], *[stk(o) for o in outs_s])
```

```python
import functools
import math

import jax
import jax.numpy as jnp
from jax import lax
from jax.experimental import pallas as pl
from jax.experimental.pallas import tpu as pltpu

D_MODEL = 1024
N_META = 16
H_A, KV_A, HD_A = 4, 2, 64
G_A = H_A // KV_A
VD_A = 2 * HD_A
H_B, KV_B, HD_B = 8, 2, 64
G_B = H_B // KV_B
H_I, D_I = 8, 64
TOPK_MAX = 256
D_C, CONV_C = 512, 31
D_FF, CONV_F = 2816, 3
ROPE_THETA = 500000.0
ROT_HALF = 8
EPS = 1e-6
LN_EPS = 1e-5

LANE = 128
SUBLANE = 8
ROW_TILE = 256
Q_TILE = 128
KEY_TILE = 256
FF_CHUNK = 256
VMEM_LIMIT = 56 * 1024 * 1024

LOG2E = 1.4426950408889634
MASKED = -1e30
M_FLOOR = -5e29
NO_SCORE = -3.0e38
INT_MIN = -(2 ** 31)
INT_MAX = 2 ** 31 - 1
KEY_F_MAX = 2147483392.0
SEARCH_CAP = 80

F32 = jnp.float32
BF16 = jnp.bfloat16
I32 = jnp.int32

C_AQ, C_KVA, C_BQ, C_BKK, C_KVB, C_IQ, C_IKK, C_IW, C_CA, C_CG, C_GZ = (
    0, 512, 1024, 1536, 1792, 2048, 2560, 2688, 2816, 3328, 3840)
N_PROJ = C_GZ + 3 * D_MODEL
R_AVT, R_BVT, R_IWT, N_PROJ_T = 0, 256, 512, 528


def _params(sem):
    return pltpu.CompilerParams(dimension_semantics=sem, vmem_limit_bytes=VMEM_LIMIT)


def _const_spec(shape):
    nd = len(shape)
    return pl.BlockSpec(shape, lambda *_: (0,) * nd)


def _rms(x, g):
    return x * lax.rsqrt(jnp.mean(x * x, axis=-1, keepdims=True) + EPS) * g


def _nt_dot(a, b):
    return lax.dot_general(a, b, (((1,), (1,)), ((), ())), preferred_element_type=F32)


def _lane_ids(shape):
    return lax.broadcasted_iota(I32, shape, len(shape) - 1)


def _row_ids(shape):
    return lax.broadcasted_iota(I32, shape, 0)


def _in_proj_kernel(x_ref, g_ref, w_ref, wt_ref, cos_ref, sa_ref, sb_ref, valid_ref,
                    aq_ref, kva_ref, kab_ref, avt_ref, bq_ref, bkk_ref, bvt_ref, kvb_ref,
                    iq_ref, ikk_ref, ikkb_ref, iw_ref, iwt_ref, u_ref, gz_ref):
    xb = _rms(x_ref[...], g_ref[...]).astype(BF16)
    cos, sa, sb = cos_ref[...], sa_ref[...], sb_ref[...]

    def proj(start, width):
        return jnp.dot(xb, w_ref[:, start:start + width], preferred_element_type=F32)

    def proj_t(start, height):
        return _nt_dot(wt_ref[start:start + height, :], xb)

    def rot(z):
        return z * cos + pltpu.roll(z, LANE - ROT_HALF, 1) * sa + pltpu.roll(z, ROT_HALF, 1) * sb

    def groups(z, n_rot):
        n = z.shape[1] // LANE
        return [rot(z[:, k * LANE:(k + 1) * LANE]) if k < n_rot else z[:, k * LANE:(k + 1) * LANE]
                for k in range(n)]

    def put(ref, k, val):
        ref[:, k * LANE:(k + 1) * LANE] = val.astype(ref.dtype)

    for k, zk in enumerate(groups(proj(C_AQ, 512), 4)):
        put(aq_ref, k, zk * (HD_A ** -0.5 * LOG2E))
    for k, zk in enumerate(groups(proj(C_KVA, 512), 2)):
        put(kva_ref, k, zk)
        if k < 2:
            put(kab_ref, k, zk)
    avt_ref[...] = proj_t(R_AVT, 256).astype(BF16)
    for k, zk in enumerate(groups(proj(C_BQ, 512), 4)):
        put(bq_ref, k, zk * (HD_B ** -0.5 * LOG2E))
    for k, zk in enumerate(groups(proj(C_BKK, 256), 2)):
        put(bkk_ref, k, zk)
    bvt_ref[...] = proj_t(R_BVT, 256).astype(BF16)
    for k, zk in enumerate(groups(proj(C_KVB, 256), 1)):
        put(kvb_ref, k, zk)
    for k, zk in enumerate(groups(proj(C_IQ, 512), 4)):
        put(iq_ref, k, zk * (D_I ** -0.5))
    ikk = rot(proj(C_IKK, LANE))
    ikk_ref[...] = ikk
    ikkb_ref[...] = ikk.astype(BF16)
    iw_ref[...] = proj(C_IW, LANE) * (H_I ** -0.5)
    iwt_ref[...] = proj_t(R_IWT, 2 * SUBLANE)[:H_I] * (H_I ** -0.5)
    u_ref[...] = proj(C_CA, D_C) * jax.nn.sigmoid(proj(C_CG, D_C)) * valid_ref[...]
    gz_ref[...] = proj(C_GZ, 3 * D_MODEL)


IN_PROJ_OUTS = ("aq", "kva", "kab", "avt", "bq", "bkk", "bvt", "kvb", "iq", "ikk", "ikkb", "iw", "iwt", "u", "gz")


def _in_proj(x, g, w, wt, cos, sa, sb, valid, table_tiles):
    n = x.shape[0]
    nt = n // ROW_TILE
    row = lambda width: pl.BlockSpec((ROW_TILE, width), lambda i: (i, 0))
    tab = pl.BlockSpec((ROW_TILE, LANE), lambda i: (i % table_tiles, 0))
    rows = lambda width, dt: (row(width), jax.ShapeDtypeStruct((n, width), dt))
    tile_t = (pl.BlockSpec((None, ROW_TILE, ROW_TILE), lambda i: (i, 0, 0)),
              jax.ShapeDtypeStruct((nt, ROW_TILE, ROW_TILE), BF16))
    outs = [rows(512, BF16), rows(512, F32), rows(256, BF16), tile_t, rows(512, BF16), rows(256, BF16), tile_t,
            rows(256, F32), rows(512, BF16), rows(LANE, F32), rows(LANE, BF16), rows(LANE, F32),
            (pl.BlockSpec((H_I, ROW_TILE), lambda i: (0, i)), jax.ShapeDtypeStruct((H_I, n), F32)),
            rows(D_C, F32), rows(3 * D_MODEL, F32)]
    res = pl.pallas_call(
        _in_proj_kernel,
        grid=(nt,),
        in_specs=[row(D_MODEL), _const_spec((1, D_MODEL)), _const_spec((D_MODEL, N_PROJ)),
                  _const_spec((N_PROJ_T, D_MODEL)),
                  tab, tab, tab, pl.BlockSpec((ROW_TILE, 1), lambda i: (i % table_tiles, 0))],
        out_specs=[o[0] for o in outs],
        out_shape=[o[1] for o in outs],
        compiler_params=_params(("parallel",)),
        name="in_proj",
    )(x, g, w, wt, cos, sa, sb, valid)
    return dict(zip(IN_PROJ_OUTS, res))


def _prep_w_in(w):
    o_aq, o_ak, o_av, o_bq, o_bk, o_bv, o_iq, o_iw, o_ik, o_cin, o_gz = (
        0, 512, 768, 1024, 1536, 1664, 1792, 2304, 2312, 2376, 3400)
    bk0, bk1 = w[:, o_bk:o_bk + 64], w[:, o_bk + 64:o_bk + 128]
    bv0, bv1 = w[:, o_bv:o_bv + 64], w[:, o_bv + 64:o_bv + 128]
    ik = w[:, o_ik:o_ik + D_I]
    iw = w[:, o_iw:o_iw + H_I]
    cols = [w[:, o_aq:o_aq + 512], w[:, o_ak:o_ak + 256], w[:, o_av:o_av + 256], w[:, o_bq:o_bq + 512],
            bk0, bk0, bk1, bk1,
            w[:, o_bk:o_bk + 128], w[:, o_bv:o_bv + 128],
            w[:, o_iq:o_iq + 512], ik, ik,
            iw, jnp.zeros((w.shape[0], LANE - H_I), w.dtype),
            w[:, o_cin:o_cin + 2 * D_C], w[:, o_gz:o_gz + 3 * D_MODEL]]
    out = jnp.concatenate(cols, axis=1).astype(BF16)
    cols_t = [w[:, o_av:o_av + 256], bv0, bv0, bv1, bv1, iw,
              jnp.zeros((w.shape[0], N_PROJ_T - R_IWT - H_I), w.dtype)]
    out_t = jnp.concatenate(cols_t, axis=1).T.astype(BF16)
    assert out.shape[1] == N_PROJ and out_t.shape[0] == N_PROJ_T
    return out, out_t


def _rotary_tables(pos):
    rot = 2 * ROT_HALF
    inv = jnp.power(ROPE_THETA, -jnp.arange(ROT_HALF, dtype=F32) * 2.0 / rot)
    ang = pos.astype(F32)[:, None] * inv[None, :]
    cos, sin = jnp.cos(ang), jnp.sin(ang)
    t = pos.shape[0]
    rest = 64 - rot
    one = jnp.ones((t, rest), F32)
    zero = jnp.zeros((t, rest), F32)
    zh = jnp.zeros((t, ROT_HALF), F32)
    c64 = jnp.concatenate([cos, cos, one], axis=1)
    sa64 = jnp.concatenate([-sin, zh, zero], axis=1)
    sb64 = jnp.concatenate([zh, sin, zero], axis=1)
    dup = lambda a: jnp.concatenate([a, a], axis=1)
    return dup(c64), dup(sa64), dup(sb64)


def _split_pair(pair):
    lo = _lane_ids(pair.shape) < 64
    zero = jnp.zeros_like(pair)
    return jnp.where(lo, pair, zero), jnp.where(lo, zero, pair)


def _monotone_key(score):
    b = lax.bitcast_convert_type(score, I32)
    return jnp.where(b < 0, b ^ I32(0x7FFFFFFF), b)


Q_PAIR = 2 * Q_TILE


def _pair_up(q_groups):
    return [jnp.concatenate(q_groups[n:n + 2], axis=0) for n in range(0, len(q_groups), 2)]


def _flash_run_t(load_k, load_vt, bias_of, q_pairs, j_last, segments):
    def logits(j):
        k = load_k(j)
        return [_nt_dot(k, q) for q in q_pairs]

    def step(j, s_list, state, masked):
        v_t = load_vt(j)
        out = []
        if masked:
            bias = bias_of(j)
            bias = jnp.concatenate([bias, bias], axis=1)
        for s, (m, l, acc) in zip(s_list, state):
            if masked:
                s = s + bias
            m_new = jnp.maximum(m, jnp.max(s, axis=0, keepdims=True))
            alpha = jnp.exp2(m - m_new)
            p = jnp.exp2(s - m_new)
            l = alpha * l + jnp.sum(p, axis=0, keepdims=True)
            acc = alpha * acc + jnp.dot(v_t, p.astype(BF16), preferred_element_type=F32)
            out.append((m_new, l, acc))
        return out

    state = [(jnp.full((1, Q_PAIR), M_FLOOR, F32), jnp.zeros((1, Q_PAIR), F32), jnp.zeros((LANE, Q_PAIR), F32))
             for _ in q_pairs]
    carry = (logits(0), state)
    for first, stop, masked in segments:
        def body(j, c, masked=masked):
            s_next = logits(jnp.minimum(j + 1, j_last))
            return s_next, step(j, c[0], c[1], masked)
        carry = lax.fori_loop(first, stop, body, carry)
    return [acc / jnp.where(l > 0.0, l, 1.0) for _, l, acc in carry[1]]


def _attn_a_kernel(lam_ref, q_ref, k_ref, vt_ref, sub_ref, o_ref, *, pad, out_scale):
    i = pl.program_id(1)
    q0 = i * Q_TILE

    @pl.when(q0 + Q_TILE <= pad)
    def _():
        o_ref[...] = jnp.zeros_like(o_ref)

    @pl.when(q0 + Q_TILE > pad)
    def _():
        lam = lam_ref[0]
        j_diag = (q0 + Q_TILE - 1) // KEY_TILE
        qpos = _lane_ids((KEY_TILE, Q_TILE)) + q0
        krow = _row_ids((KEY_TILE, Q_TILE))
        for kv in range(KV_A):
            q_groups = []
            for g in range(G_A):
                c0 = (kv * G_A + g) * LANE
                q_groups += list(_split_pair(q_ref[:, c0:c0 + LANE]))

            def load_k(j, kv=kv):
                return k_ref[pl.ds(pl.multiple_of(j * KEY_TILE, KEY_TILE), KEY_TILE), kv * LANE:(kv + 1) * LANE]

            def load_vt(j, kv=kv):
                return vt_ref[j, kv * LANE:(kv + 1) * LANE, :]

            def bias_of(j):
                kpos = krow + j * KEY_TILE
                return jnp.where((kpos <= qpos) & (kpos >= pad), 0.0, MASKED)

            segments = ((0, 1, True), (1, j_diag, False), (jnp.maximum(j_diag, 1), j_diag + 1, True))
            o = _flash_run_t(load_k, load_vt, bias_of, _pair_up(q_groups), j_diag, segments)
            for g in range(G_A):
                y = o[g][:, :Q_TILE] - lam * o[g][:, Q_TILE:]
                y = y * lax.rsqrt(jnp.mean(y * y, axis=0, keepdims=True) + EPS) * sub_ref[...] * out_scale
                c0 = (kv * G_A + g) * LANE
                o_ref[:, c0:c0 + LANE] = y.T.astype(o_ref.dtype)


def _attn_a_prompt(lam, aq, kab, avt, subln_col, pad, out_scale):
    b, tp, _ = aq.shape
    return pl.pallas_call(
        functools.partial(_attn_a_kernel, pad=pad, out_scale=out_scale),
        grid=(b, tp // Q_TILE),
        in_specs=[pl.BlockSpec(memory_space=pltpu.SMEM),
                  pl.BlockSpec((None, Q_TILE, 512), lambda bi, i: (bi, i, 0)),
                  pl.BlockSpec((None, tp, 256), lambda bi, i: (bi, 0, 0)),
                  pl.BlockSpec((None, tp // KEY_TILE, 256, KEY_TILE), lambda bi, i: (bi, 0, 0, 0)),
                  _const_spec((LANE, 1))],
        out_specs=pl.BlockSpec((None, Q_TILE, 512), lambda bi, i: (bi, i, 0)),
        out_shape=jax.ShapeDtypeStruct((b, tp, 512), BF16),
        compiler_params=_params(("parallel", "parallel")),
        name="attn_a_prompt",
    )(lam, aq, kab, avt, subln_col)


def _kth_largest_key(count_ge, k_sel, n_vis, score_min, score_max):
    kf = float(k_sel)
    key_min, key_max = _monotone_key(score_min), _monotone_key(score_max)
    no_key = _monotone_key(jnp.full(n_vis.shape, NO_SCORE, F32))
    zero = jnp.zeros_like(key_min)
    one = zero + 1
    c_pos = count_ge(one)
    c_nn = count_ge(zero)
    few = n_vis <= kf
    pos = c_pos >= kf
    neg = c_nn < kf
    lo = jnp.where(pos, one, jnp.where(neg, key_min, zero))
    hi = jnp.where(pos, key_max + 1, jnp.where(neg, zero, one))
    c_lo = jnp.where(pos, c_pos, jnp.where(neg, n_vis, c_nn))
    c_hi = jnp.where(pos, 0.0, jnp.where(neg, c_nn, c_pos))
    lo = jnp.where(few, no_key + 1, lo)
    done = few | (c_lo == kf) | (hi - 1 <= lo)
    ones = jnp.ones_like(n_vis)

    def cond(st):
        it, active = st[0], st[-1]
        return (it < SEARCH_CAP) & (jnp.sum(active) > 0.0)

    def body(st):
        it, lo, hi, c_lo, c_hi, f_lo, f_hi, last, active = st
        done = active == 0.0
        g_lo = (c_lo - (kf - 0.5)) * f_lo
        g_hi = ((kf - 0.5) - c_hi) * f_hi
        frac = jnp.where(done, 0.5, g_hi / (g_lo + g_hi))
        lo_f, hi_f = lo.astype(F32), hi.astype(F32)
        t_f = jnp.clip(hi_f - (hi_f - lo_f) * frac, -KEY_F_MAX, KEY_F_MAX)
        t_mid = (lo >> 1) + (hi >> 1) + (lo & hi & 1)
        t = jnp.where((f_lo < 0.2) | (f_hi < 0.2), t_mid, t_f.astype(I32))
        t = jnp.minimum(jnp.maximum(t, lo + 1), hi - 1)
        c = count_ge(t)
        up = c >= kf
        act_up = jnp.logical_and(jnp.logical_not(done), up)
        act_dn = jnp.logical_and(jnp.logical_not(done), jnp.logical_not(up))
        f_lo = jnp.where(act_dn & (last == -1), f_lo * 0.5, jnp.where(act_up, ones, f_lo))
        f_hi = jnp.where(act_up & (last == 1), f_hi * 0.5, jnp.where(act_dn, ones, f_hi))
        last = jnp.where(act_up, 1, jnp.where(act_dn, -1, last))
        lo = jnp.where(act_up, t, lo)
        c_lo = jnp.where(act_up, c, c_lo)
        hi = jnp.where(act_dn, t, hi)
        c_hi = jnp.where(act_dn, c, c_hi)
        done = done | (c_lo == kf) | (hi - 1 <= lo)
        return it + 1, lo, hi, c_lo, c_hi, f_lo, f_hi, last, jnp.where(done, 0.0, 1.0)

    st = lax.while_loop(cond, body, (I32(0), lo, hi, c_lo, c_hi, ones, ones, zero, jnp.where(done, 0.0, 1.0)))
    return st[1]


def _attn_b_kernel(iq_ref, iwt_ref, q_ref, ik_ref, kk_ref, vvt_ref, o_ref, key_ref, *, pad, k_sel):
    i = pl.program_id(1)
    q0 = i * Q_TILE

    @pl.when(q0 + Q_TILE <= pad)
    def _():
        o_ref[...] = jnp.zeros_like(o_ref)

    @pl.when(q0 + Q_TILE > pad)
    def _():
        j_diag = (q0 + Q_TILE - 1) // KEY_TILE
        n_kb = j_diag + 1
        qpos = _lane_ids((KEY_TILE, Q_TILE)) + q0
        krow = _row_ids((KEY_TILE, Q_TILE))
        no_key = _monotone_key(jnp.full((1, 1), NO_SCORE, F32))
        def fold(a, op):
            parts = [a[r * SUBLANE:(r + 1) * SUBLANE] for r in range(KEY_TILE // SUBLANE)]
            while len(parts) > 1:
                parts = [op(parts[n], parts[n + 1]) for n in range(0, len(parts), 2)]
            return parts[0]

        parts = []
        for pr in range(H_I // 2):
            parts += list(_split_pair(iq_ref[:, pr * LANE:(pr + 1) * LANE]))
        qi = jnp.concatenate(parts, axis=0)
        iwt = iwt_ref[...]

        def score_tile(j, carry, masked):
            kmin, kmax = carry
            k0 = pl.multiple_of(j * KEY_TILE, KEY_TILE)
            d = jnp.maximum(_nt_dot(ik_ref[pl.ds(k0, KEY_TILE), :], qi), 0.0)
            sc = jnp.zeros((KEY_TILE, Q_TILE), F32)
            for h in range(H_I):
                sc = sc + iwt[h:h + 1, :] * d[:, h * Q_TILE:(h + 1) * Q_TILE]
            sc = sc + 0.0
            sc_lo = sc_hi = sc
            key = _monotone_key(sc)
            if masked:
                kpos = krow + k0
                vis = (kpos <= qpos) & (kpos >= pad)
                key = jnp.where(vis, key, no_key)
                sc_lo = jnp.where(vis, sc, -NO_SCORE)
                sc_hi = jnp.where(vis, sc, NO_SCORE)
            key_ref[j] = key
            return jnp.minimum(kmin, fold(sc_lo, jnp.minimum)), jnp.maximum(kmax, fold(sc_hi, jnp.maximum))

        carry = (jnp.full((SUBLANE, Q_TILE), -NO_SCORE, F32), jnp.full((SUBLANE, Q_TILE), NO_SCORE, F32))
        carry = score_tile(0, carry, True)
        carry = lax.fori_loop(1, j_diag, lambda j, c: score_tile(j, c, False), carry)
        carry = lax.fori_loop(jnp.maximum(j_diag, 1), n_kb, lambda j, c: score_tile(j, c, True), carry)
        sc_min = jnp.min(carry[0], axis=0, keepdims=True)
        sc_max = jnp.max(carry[1], axis=0, keepdims=True)

        def count_ge(t):
            def cbody(j, c):
                return c + fold(jnp.where(key_ref[j] >= t, 1.0, 0.0), jnp.add)
            c = lax.fori_loop(0, n_kb, cbody, jnp.zeros((SUBLANE, Q_TILE), F32))
            return jnp.sum(c, axis=0, keepdims=True)

        n_vis = (_lane_ids((1, Q_TILE)) + (q0 - pad + 1)).astype(F32)
        thr = _kth_largest_key(count_ge, k_sel, n_vis, sc_min, sc_max)

        for kv in range(KV_B):
            q_groups = []
            for pr in range(G_B // 2):
                c0 = (kv * (G_B // 2) + pr) * LANE
                q_groups += list(_split_pair(q_ref[:, c0:c0 + LANE]))

            def load_k(j, kv=kv):
                return kk_ref[pl.ds(pl.multiple_of(j * KEY_TILE, KEY_TILE), KEY_TILE), kv * LANE:(kv + 1) * LANE]

            def load_vt(j, kv=kv):
                return vvt_ref[j, kv * LANE:(kv + 1) * LANE, :]

            def bias_of(j):
                return jnp.where(key_ref[j] >= thr, 0.0, MASKED)

            o = _flash_run_t(load_k, load_vt, bias_of, _pair_up(q_groups), j_diag, ((0, n_kb, True),))
            lo = _lane_ids((Q_TILE, LANE)) < 64
            for pr in range(G_B // 2):
                pair = jnp.where(lo, o[pr][:, :Q_TILE].T, o[pr][:, Q_TILE:].T)
                c0 = (kv * (G_B // 2) + pr) * LANE
                o_ref[:, c0:c0 + LANE] = pair.astype(o_ref.dtype)


def _attn_b_prompt(iq, iwt, bq, ikkb, bkk, bvt, pad, k_sel):
    b, tp, _ = bq.shape
    nq = tp // Q_TILE
    qspec = lambda width: pl.BlockSpec((None, Q_TILE, width), lambda bi, i: (bi, i, 0))
    full = lambda width: pl.BlockSpec((None, tp, width), lambda bi, i: (bi, 0, 0))
    return pl.pallas_call(
        functools.partial(_attn_b_kernel, pad=pad, k_sel=k_sel),
        grid=(b, nq),
        in_specs=[qspec(512), pl.BlockSpec((H_I, Q_TILE), lambda bi, i: (0, bi * nq + i)), qspec(512),
                  full(LANE), full(256),
                  pl.BlockSpec((None, tp // KEY_TILE, 256, KEY_TILE), lambda bi, i: (bi, 0, 0, 0))],
        out_specs=qspec(512),
        out_shape=jax.ShapeDtypeStruct((b, tp, 512), BF16),
        scratch_shapes=[pltpu.VMEM((tp // KEY_TILE, KEY_TILE, Q_TILE), I32)],
        compiler_params=_params(("parallel", "parallel")),
        name="attn_b_prompt",
    )(iq, iwt, bq, ikkb, bkk, bvt)


def _ln_swish(y, g, b):
    mu = jnp.mean(y, axis=-1, keepdims=True)
    yc = y - mu
    z = yc * lax.rsqrt(jnp.mean(yc * yc, axis=-1, keepdims=True) + LN_EPS) * g + b
    return z * jax.nn.sigmoid(z)


HALO = 32


def _conv_c_prompt_kernel(prev_ref, cur_ref, w_ref, b_ref, g_ref, beta_ref, o_ref, ext_ref):
    ext_ref[0:HALO, :] = prev_ref[ROW_TILE - HALO:ROW_TILE, :]
    ext_ref[HALO:HALO + ROW_TILE, :] = cur_ref[...]
    acc = jnp.zeros((ROW_TILE, D_C), F32) + b_ref[...]
    base = HALO - (CONV_C - 1)
    for j in range(CONV_C):
        acc = acc + w_ref[j:j + 1, :] * ext_ref[base + j:base + j + ROW_TILE, :]
    o_ref[...] = _ln_swish(acc, g_ref[...], beta_ref[...]).astype(o_ref.dtype)


def _conv_c_prompt(u, w, b, g, beta):
    n = u.shape[0]
    return pl.pallas_call(
        _conv_c_prompt_kernel,
        grid=(n // ROW_TILE,),
        in_specs=[pl.BlockSpec((ROW_TILE, D_C), lambda i: (jnp.maximum(i - 1, 0), 0)),
                  pl.BlockSpec((ROW_TILE, D_C), lambda i: (i, 0)),
                  _const_spec((CONV_C, D_C)), _const_spec((1, D_C)), _const_spec((1, D_C)),
                  _const_spec((1, D_C))],
        out_specs=pl.BlockSpec((ROW_TILE, D_C), lambda i: (i, 0)),
        out_shape=jax.ShapeDtypeStruct((n, D_C), BF16),
        scratch_shapes=[pltpu.VMEM((HALO + ROW_TILE, D_C), F32)],
        compiler_params=_params(("parallel",)),
        name="conv_c_prompt",
    )(u, u, w, b, g, beta)


def _conv_c_sample_kernel(st_ref, u_ref, w_ref, b_ref, g_ref, beta_ref, o_ref):
    n_hist = st_ref.shape[0]
    for t in range(u_ref.shape[0]):
        acc = jnp.zeros(o_ref.shape[1:], F32) + b_ref[...]
        for j in range(CONV_C):
            e = t + j
            row = st_ref[e] if e < n_hist else u_ref[e - n_hist]
            acc = acc + w_ref[j:j + 1, :] * row
        o_ref[t] = _ln_swish(acc, g_ref[...], beta_ref[...]).astype(o_ref.dtype)


def _conv_c_sample(st_t, u_t, w, b, g, beta):
    n_new, n_seq, _ = u_t.shape
    return pl.pallas_call(
        _conv_c_sample_kernel,
        grid=(1,),
        in_specs=[_const_spec(st_t.shape), _const_spec(u_t.shape), _const_spec((CONV_C, D_C)),
                  _const_spec((1, D_C)), _const_spec((1, D_C)), _const_spec((1, D_C))],
        out_specs=_const_spec((n_new, n_seq, D_C)),
        out_shape=jax.ShapeDtypeStruct((n_new, n_seq, D_C), BF16),
        compiler_params=_params(("arbitrary",)),
        name="conv_c_sample",
    )(st_t, u_t, w, b, g, beta)


def _merge_kernel(ya_ref, yb_ref, yc_ref, gz_ref, x_ref, wa_ref, wb_ref, wc_ref, wo_ref, o_ref):
    merged = jnp.zeros((ROW_TILE, D_MODEL), F32)
    for k, (y_ref, w_ref) in enumerate(((ya_ref, wa_ref), (yb_ref, wb_ref), (yc_ref, wc_ref))):
        gate = jax.nn.sigmoid(gz_ref[:, k * D_MODEL:(k + 1) * D_MODEL])
        merged = merged + gate * jnp.dot(y_ref[...], w_ref[...], preferred_element_type=F32)
    o_ref[...] = x_ref[...] + jnp.dot(merged.astype(BF16), wo_ref[...], preferred_element_type=F32)


def _merge(ya, yb, yc, gz, x, wa, wb, wc, wo):
    n = x.shape[0]
    row = lambda width: pl.BlockSpec((ROW_TILE, width), lambda i: (i, 0))
    return pl.pallas_call(
        _merge_kernel,
        grid=(n // ROW_TILE,),
        in_specs=[row(512), row(512), row(D_C), row(3 * D_MODEL), row(D_MODEL),
                  _const_spec(wa.shape), _const_spec(wb.shape), _const_spec(wc.shape), _const_spec(wo.shape)],
        out_specs=row(D_MODEL),
        out_shape=jax.ShapeDtypeStruct((n, D_MODEL), F32),
        compiler_params=_params(("parallel",)),
        name="merge",
    )(ya, yb, yc, gz, x, wa, wb, wc, wo)


TAIL = 8


def _ffn_kernel(*refs, carried, final_norm, n_new):
    if carried:
        (x_ref, g_ref, wu_ref, cw_ref, cb_ref, wd_ref, gf_ref, valid_ref,
         o_ref, tail_ref, carry_ref) = refs
    else:
        (x_ref, g_ref, wu_ref, cw_ref, cb_ref, wd_ref, gf_ref, h1_ref, h2_ref,
         o_ref, up_ref) = refs
    x = x_ref[...]
    xb = _rms(x, g_ref[...]).astype(BF16)
    row = _row_ids((ROW_TILE, 1))

    if carried:
        @pl.when(pl.program_id(0) == 0)
        def _():
            carry_ref[...] = jnp.zeros_like(carry_ref)
        valid = valid_ref[...]
    else:
        t_in_seq = row % n_new

    def conv(c0):
        up = jnp.dot(xb, wu_ref[:, c0:c0 + FF_CHUNK], preferred_element_type=F32)
        if carried:
            up = up * valid
            old = carry_ref[:, c0:c0 + FF_CHUNK]
            p1 = jnp.where(row == 0, old[TAIL - 1:TAIL], pltpu.roll(up, 1, 0))
            p2 = jnp.where(row == 0, old[TAIL - 2:TAIL - 1],
                           jnp.where(row == 1, old[TAIL - 1:TAIL], pltpu.roll(up, 2, 0)))
            carry_ref[:, c0:c0 + FF_CHUNK] = up[ROW_TILE - TAIL:]
            tail_ref[:, c0:c0 + FF_CHUNK] = up[ROW_TILE - TAIL:]
        else:
            p1 = jnp.where(t_in_seq < 1, h1_ref[:, c0:c0 + FF_CHUNK], pltpu.roll(up, 1, 0))
            p2 = jnp.where(t_in_seq < 2, h2_ref[:, c0:c0 + FF_CHUNK], pltpu.roll(up, 2, 0))
            up_ref[:, c0:c0 + FF_CHUNK] = up
        w = cw_ref[:, c0:c0 + FF_CHUNK]
        return w[0:1] * p2 + w[1:2] * p1 + w[2:3] * up + cb_ref[:, c0:c0 + FF_CHUNK]

    acc = jnp.zeros((ROW_TILE, D_MODEL), F32)
    for c in range(D_FF // FF_CHUNK):
        hg = conv(c * FF_CHUNK)
        hv = conv(D_FF + c * FF_CHUNK)
        h = (hg * jax.nn.sigmoid(hg) * hv).astype(BF16)
        acc = acc + jnp.dot(h, wd_ref[c * FF_CHUNK:(c + 1) * FF_CHUNK, :], preferred_element_type=F32)
    y = x + acc
    o_ref[...] = _rms(y, gf_ref[...]) if final_norm else y


def _ffn(x, g, wu, cw, cb, wd, gf, extra, carried, final_norm, n_new=1):
    n = x.shape[0]
    nt = n // ROW_TILE
    row = lambda width: pl.BlockSpec((ROW_TILE, width), lambda i: (i, 0))
    in_specs = [row(D_MODEL), _const_spec((1, D_MODEL)), _const_spec(wu.shape), _const_spec(cw.shape),
                _const_spec(cb.shape), _const_spec(wd.shape), _const_spec((1, D_MODEL))]
    if carried:
        in_specs += [pl.BlockSpec((ROW_TILE, 1), lambda i: (i, 0))]
        out_specs = [row(D_MODEL), pl.BlockSpec((None, TAIL, 2 * D_FF), lambda i: (i, 0, 0))]
        out_shape = [jax.ShapeDtypeStruct((n, D_MODEL), F32), jax.ShapeDtypeStruct((nt, TAIL, 2 * D_FF), F32)]
        scratch = [pltpu.VMEM((TAIL, 2 * D_FF), F32)]
    else:
        in_specs += [row(2 * D_FF), row(2 * D_FF)]
        out_specs = [row(D_MODEL), row(2 * D_FF)]
        out_shape = [jax.ShapeDtypeStruct((n, D_MODEL), F32), jax.ShapeDtypeStruct((n, 2 * D_FF), F32)]
        scratch = []
    return pl.pallas_call(
        functools.partial(_ffn_kernel, carried=carried, final_norm=final_norm, n_new=n_new),
        grid=(nt,),
        in_specs=in_specs, out_specs=out_specs, out_shape=out_shape, scratch_shapes=scratch,
        compiler_params=_params(("arbitrary",)),
        name="ffn_prompt" if carried else "ffn_sample",
    )(x, g, wu, cw, cb, wd, gf, *extra)


S_ROWS = 8


def _new_key_mask(shape, n_past, n_new):
    col = _lane_ids(shape)
    t = _row_ids(shape) % S_ROWS
    return (col < n_past) | ((col - n_past <= t) & (col < n_past + n_new))


def _pad_rows(a, rows):
    return jnp.concatenate([a, jnp.zeros((rows - a.shape[0], a.shape[1]), a.dtype)], axis=0)


def _page_specs(block, layer, n_pages):
    nd = len(block)
    return [pl.BlockSpec((None, None) + block,
                         lambda s, pt, j=j: (layer, pt[s * n_pages + j]) + (0,) * nd) for j in range(n_pages)]


def _attn_a_sample_kernel(pt_ref, lam_ref, q_ref, new_ref, sub_ref, *rest, n_pages, page, n_new, out_scale):
    page_refs, o_ref = rest[:n_pages], rest[n_pages]
    lam = lam_ref[0]
    n_past = n_pages * page
    q = q_ref[...].astype(F32)
    new = _pad_rows(new_ref[...], LANE)
    for kv in range(KV_A):
        parts = []
        for g in range(G_A):
            c0 = (kv * G_A + g) * LANE
            parts += list(_split_pair(q[:, c0:c0 + LANE]))
        qs = jnp.concatenate(parts, axis=0).astype(BF16)
        ksl = slice(kv * LANE, (kv + 1) * LANE)
        vsl = slice((KV_A + kv) * LANE, (KV_A + kv + 1) * LANE)
        k = jnp.concatenate([r[pl.ds(kv, page, stride=2 * KV_A), :].astype(BF16) for r in page_refs]
                            + [new[:, ksl].astype(BF16)], axis=0)
        v = jnp.concatenate([r[pl.ds(KV_A + kv, page, stride=2 * KV_A), :].astype(BF16) for r in page_refs]
                            + [new[:, vsl].astype(BF16)], axis=0)
        s = _nt_dot(qs, k)
        s = jnp.where(_new_key_mask(s.shape, n_past, n_new), s, MASKED)
        p = jnp.exp2(s - jnp.max(s, axis=1, keepdims=True))
        o = jnp.dot(p.astype(BF16), v, preferred_element_type=F32) / jnp.sum(p, axis=1, keepdims=True)
        for g in range(G_A):
            b0 = g * 2 * S_ROWS
            y = o[b0:b0 + S_ROWS] - lam * o[b0 + S_ROWS:b0 + 2 * S_ROWS]
            y = _rms(y, sub_ref[...]) * out_scale
            c0 = (kv * G_A + g) * LANE
            o_ref[:, c0:c0 + LANE] = y.astype(o_ref.dtype)


def _attn_a_sample(pt, lam, aq, new, subln, cache, layer, n_new, out_scale):
    n_seq = aq.shape[0]
    n_pages = pt.shape[0] // n_seq
    page = cache.shape[2] // (2 * KV_A)
    seq = lambda width: pl.BlockSpec((None, S_ROWS, width), lambda s, pt: (s, 0, 0))
    grid_spec = pltpu.PrefetchScalarGridSpec(
        num_scalar_prefetch=1,
        grid=(n_seq,),
        in_specs=[pl.BlockSpec(memory_space=pltpu.SMEM), seq(512), seq(512),
                  pl.BlockSpec((1, LANE), lambda s, pt: (0, 0))]
        + _page_specs((page * 2 * KV_A, LANE), layer, n_pages),
        out_specs=seq(512),
    )
    return pl.pallas_call(
        functools.partial(_attn_a_sample_kernel, n_pages=n_pages, page=page, n_new=n_new, out_scale=out_scale),
        grid_spec=grid_spec,
        out_shape=jax.ShapeDtypeStruct((n_seq, S_ROWS, 512), BF16),
        compiler_params=_params(("parallel",)),
        name="attn_a_sample",
    )(pt, lam, aq, new, subln, *([cache] * n_pages))


def _attn_b_sample_kernel(pt_ref, iq_ref, iw_ref, q_ref, newkv_ref, newik_ref, *rest,
                          n_pages, page, n_new, k_sel):
    kv_refs, ik_refs, o_ref = rest[:n_pages], rest[n_pages:2 * n_pages], rest[2 * n_pages]
    n_past = n_pages * page
    lo = _lane_ids((S_ROWS, LANE)) < 64

    def heads64(x):
        out = []
        for pr in range(x.shape[1] // LANE):
            pair = x[:, pr * LANE:(pr + 1) * LANE]
            out += [pair[:, :64], pltpu.roll(pair, 64, 1)[:, :64]]
        return out

    qi = jnp.concatenate(heads64(iq_ref[...].astype(F32)), axis=0).astype(BF16)
    ik_t = jnp.concatenate([r[...].astype(BF16) for r in ik_refs], axis=1)
    newik = _pad_rows(newik_ref[:, :D_I], LANE).astype(BF16)
    d = jnp.concatenate([jnp.dot(qi, ik_t, preferred_element_type=F32), _nt_dot(qi, newik)], axis=1)
    d = jnp.maximum(d, 0.0)
    iw = iw_ref[...]
    sc = jnp.zeros((S_ROWS, d.shape[1]), F32)
    for h in range(H_I):
        sc = sc + iw[:, h:h + 1] * d[h * S_ROWS:(h + 1) * S_ROWS]
    sc = sc + 0.0
    vis = _new_key_mask(sc.shape, n_past, n_new)
    no_key = _monotone_key(jnp.full((1, 1), NO_SCORE, F32))
    keys = jnp.where(vis, _monotone_key(sc), no_key)

    def count_ge(t):
        return jnp.sum(jnp.where(keys >= t, 1.0, 0.0), axis=1, keepdims=True)

    n_vis = jnp.sum(jnp.where(vis, 1.0, 0.0), axis=1, keepdims=True)
    sc_min = jnp.min(jnp.where(vis, sc, -NO_SCORE), axis=1, keepdims=True)
    sc_max = jnp.max(jnp.where(vis, sc, NO_SCORE), axis=1, keepdims=True)
    thr = _kth_largest_key(count_ge, k_sel, n_vis, sc_min, sc_max)
    bias = jnp.concatenate([jnp.where(keys >= thr, 0.0, MASKED)] * G_B, axis=0)

    q_heads = heads64(q_ref[...].astype(F32))
    newkv = _pad_rows(newkv_ref[...], LANE)
    newk128, newv128 = newkv[:, 0:LANE], newkv[:, LANE:2 * LANE]
    for kv in range(KV_B):
        qs = jnp.concatenate(q_heads[kv * G_B:(kv + 1) * G_B], axis=0).astype(BF16)
        k_t = jnp.concatenate([r[kv * 64:(kv + 1) * 64, :].astype(BF16) for r in kv_refs], axis=1)
        v_t = jnp.concatenate([r[(KV_B + kv) * 64:(KV_B + kv + 1) * 64, :].astype(BF16) for r in kv_refs],
                              axis=1)
        v_t2 = jnp.concatenate([v_t, v_t], axis=0)
        newk = (newk128 if kv == 0 else pltpu.roll(newk128, 64, 1))[:, :64].astype(BF16)
        s = jnp.concatenate([jnp.dot(qs, k_t, preferred_element_type=F32), _nt_dot(qs, newk)], axis=1) + bias
        m = jnp.maximum(jnp.max(s, axis=1, keepdims=True), M_FLOOR)
        p = jnp.exp2(s - m)
        l = jnp.sum(p, axis=1, keepdims=True)
        pb = p.astype(BF16)
        o_new = jnp.dot(pb[:, n_past:], newv128.astype(BF16), preferred_element_type=F32)
        swapped = pltpu.roll(o_new, 64, 1)
        lo4 = jnp.concatenate([lo] * G_B, axis=0)
        o_new = jnp.where(lo4, o_new, swapped) if kv == 0 else jnp.where(lo4, swapped, o_new)
        o = (_nt_dot(pb[:, :n_past], v_t2) + o_new) / jnp.where(l > 0.0, l, 1.0)
        for pr in range(G_B // 2):
            even = o[2 * pr * S_ROWS:(2 * pr + 1) * S_ROWS]
            odd = o[(2 * pr + 1) * S_ROWS:(2 * pr + 2) * S_ROWS]
            c0 = (kv * (G_B // 2) + pr) * LANE
            o_ref[:, c0:c0 + LANE] = jnp.where(lo, even, odd).astype(o_ref.dtype)


def _attn_b_sample(pt, iq, iw, bq, newkv, newik, cache_kv_t, cache_ik_t, layer, n_new, k_sel):
    n_seq = bq.shape[0]
    n_pages = pt.shape[0] // n_seq
    page = cache_kv_t.shape[3]
    seq = lambda width: pl.BlockSpec((None, S_ROWS, width), lambda s, pt: (s, 0, 0))
    grid_spec = pltpu.PrefetchScalarGridSpec(
        num_scalar_prefetch=1,
        grid=(n_seq,),
        in_specs=[seq(512), seq(LANE), seq(512), seq(256), seq(LANE)]
        + _page_specs((2 * KV_B * HD_B, page), layer, n_pages) + _page_specs((D_I, page), layer, n_pages),
        out_specs=seq(512),
    )
    return pl.pallas_call(
        functools.partial(_attn_b_sample_kernel, n_pages=n_pages, page=page, n_new=n_new, k_sel=k_sel),
        grid_spec=grid_spec,
        out_shape=jax.ShapeDtypeStruct((n_seq, S_ROWS, 512), BF16),
        compiler_params=_params(("parallel",)),
        name="attn_b_sample",
    )(pt, iq, iw, bq, newkv, newik, *([cache_kv_t] * n_pages), *([cache_ik_t] * n_pages))


def kernel(x_prompt, x_sample, cache_kv_a, cache_kv_b, cache_idx_b, state_conv_c, state_conv_ffn, page_table, meta_tokens, norm_mix, w_in, lam_q1, lam_k1, lam_q2, lam_k2, subln_a, w_out_a, w_out_b, conv_c_w, conv_c_b, ln_c_g, ln_c_b, w_out_c, w_o, norm_ffn, w_up, conv_f_w, conv_f_b, w_down, norm_final):
    depth = w_in.shape[0]
    n_prompt, seq, _ = x_prompt.shape
    t_len = seq + N_META
    pad = (-t_len) % ROW_TILE
    tp = t_len + pad
    n_seq, n_new, _ = x_sample.shape
    n_pool, page = cache_kv_a.shape[1:3]
    n_pages = page_table.shape[1]
    past_len = n_pages * page
    k_sel_p = min(TOPK_MAX, seq // 4)
    k_sel_s = min(TOPK_MAX, (past_len + n_new) // 4)
    assert n_new <= S_ROWS and ROW_TILE % n_new == 0 and (n_seq * n_new) % ROW_TILE == 0
    assert tp % ROW_TILE == 0 and KEY_TILE == ROW_TILE and page == LANE

    xp = jnp.concatenate([jnp.zeros((n_prompt, pad, D_MODEL), F32),
                          jnp.broadcast_to(meta_tokens[None], (n_prompt, N_META, D_MODEL)), x_prompt], axis=1)
    xp = xp.reshape(n_prompt * tp, D_MODEL)
    xs = x_sample.reshape(n_seq * n_new, D_MODEL)
    tab_p = _rotary_tables(jnp.arange(tp, dtype=I32) - pad)
    tab_s = _rotary_tables(past_len + (jnp.arange(ROW_TILE, dtype=I32) % n_new))
    valid_p = (jnp.arange(tp) >= pad).astype(F32)[:, None]
    valid_s = jnp.ones((ROW_TILE, 1), F32)
    valid_p_flat = jnp.tile(valid_p, (n_prompt, 1))
    pt_flat = page_table.reshape(-1).astype(I32)
    cache_a = cache_kv_a.reshape(depth, n_pool, page * 2 * KV_A, 2 * HD_A)
    cache_b_t = jnp.transpose(cache_kv_b, (0, 1, 3, 4, 5, 2)).reshape(depth, n_pool, 2 * KV_B * HD_B, page)
    cache_i_t = jnp.transpose(cache_idx_b, (0, 1, 3, 2))

    row1 = lambda a: a.reshape(1, -1)
    pad_seq = lambda a: jnp.pad(a.reshape(n_seq, n_new, -1), ((0, 0), (0, S_ROWS - n_new), (0, 0)))
    unpad_seq = lambda a: a[:, :n_new].reshape(n_seq * n_new, -1)
    per_seq = lambda a: a.reshape(n_prompt, tp, a.shape[-1])
    tiles_t = lambda a: a.reshape(n_prompt, tp // KEY_TILE, ROW_TILE, KEY_TILE)

    outs_p = [[] for _ in range(5)]
    outs_s = [[] for _ in range(5)]
    for l in range(depth):
        lam_init = 0.8 - 0.6 * math.exp(-0.3 * l)
        lam = (jnp.exp(jnp.sum(lam_q1[l] * lam_k1[l])) - jnp.exp(jnp.sum(lam_q2[l] * lam_k2[l]))
               + lam_init).reshape(1).astype(F32)
        out_scale = 1.0 - lam_init
        last = l == depth - 1
        w_in_l, w_in_t = _prep_w_in(w_in[l])
        wa, wb, wc, wo = (w_out_a[l].astype(BF16), w_out_b[l].astype(BF16), w_out_c[l].astype(BF16),
                          w_o[l].astype(BF16))
        wu, wd = w_up[l].astype(BF16), w_down[l].astype(BF16)
        ffn_w = (row1(norm_ffn[l]), wu, conv_f_w[l], row1(conv_f_b[l]), wd, row1(norm_final))
        conv_w = (conv_c_w[l], row1(conv_c_b[l]), row1(ln_c_g[l]), row1(ln_c_b[l]))

        z = _in_proj(xp, row1(norm_mix[l]), w_in_l, w_in_t, *tab_p, valid_p, tp // ROW_TILE)
        ya = _attn_a_prompt(lam, per_seq(z["aq"]), per_seq(z["kab"]), tiles_t(z["avt"]),
                            subln_a[l].reshape(-1, 1), pad, out_scale)
        yb = _attn_b_prompt(per_seq(z["iq"]), z["iwt"], per_seq(z["bq"]), per_seq(z["ikkb"]), per_seq(z["bkk"]),
                            tiles_t(z["bvt"]), pad, k_sel_p)
        yc = _conv_c_prompt(z["u"], *conv_w)
        xp = _merge(ya.reshape(-1, 512), yb.reshape(-1, 512), yc, z["gz"], xp, wa, wb, wc, wo)
        xp, tails = _ffn(xp, *ffn_w, (valid_p_flat,), carried=True, final_norm=last)
        outs_p[0].append(per_seq(z["kva"])[:, pad:].reshape(n_prompt, t_len, 2, KV_A, 2 * HD_A))
        outs_p[1].append(per_seq(z["kvb"])[:, pad:].reshape(n_prompt, t_len, 2, KV_B, HD_B))
        outs_p[2].append(per_seq(z["ikk"])[:, pad:, :D_I])
        outs_p[3].append(per_seq(z["u"])[:, tp - (CONV_C - 1):])
        tails = tails.reshape(n_prompt, tp // ROW_TILE, TAIL, 2 * D_FF)
        outs_p[4].append(tails[:, -1, TAIL - (CONV_F - 1):])

        z = _in_proj(xs, row1(norm_mix[l]), w_in_l, w_in_t, *tab_s, valid_s, 1)
        ya = _attn_a_sample(pt_flat, lam, pad_seq(z["aq"]), pad_seq(z["kva"]), row1(subln_a[l]), cache_a, l,
                            n_new, out_scale)
        yb = _attn_b_sample(pt_flat, pad_seq(z["iq"]), pad_seq(z["iw"]), pad_seq(z["bq"]), pad_seq(z["kvb"]),
                            pad_seq(z["ikk"]), cache_b_t, cache_i_t, l, n_new, k_sel_s)
        u_seq = z["u"].reshape(n_seq, n_new, D_C)
        yc = _conv_c_sample(jnp.swapaxes(state_conv_c[l], 0, 1), jnp.swapaxes(u_seq, 0, 1), *conv_w)
        yc = jnp.swapaxes(yc, 0, 1).reshape(n_seq * n_new, D_C)
        xs = _merge(unpad_seq(ya), unpad_seq(yb), yc, z["gz"], xs, wa, wb, wc, wo)
        st_f = state_conv_ffn[l]
        zrow = jnp.zeros_like(st_f[:, :1])
        h1 = jnp.concatenate([st_f[:, 1:2]] + [zrow] * (n_new - 1), axis=1).reshape(n_seq * n_new, 2 * D_FF)
        h2 = jnp.concatenate([st_f[:, 0:1], st_f[:, 1:2]] + [zrow] * (n_new - 2), axis=1).reshape(
            n_seq * n_new, 2 * D_FF)
        xs, up = _ffn(xs, *ffn_w, (h1, h2), carried=False, final_norm=last, n_new=n_new)
        outs_s[0].append(z["kva"].reshape(n_seq, n_new, 2, KV_A, 2 * HD_A))
        outs_s[1].append(z["kvb"].reshape(n_seq, n_new, 2, KV_B, HD_B))
        outs_s[2].append(z["ikk"][:, :D_I].reshape(n_seq, n_new, D_I))
        outs_s[3].append(jnp.concatenate([state_conv_c[l], u_seq], axis=1)[:, -(CONV_C - 1):])
        up_ext = jnp.concatenate([st_f, up.reshape(n_seq, n_new, 2 * D_FF)], axis=1)
        outs_s[4].append(up_ext[:, -(CONV_F - 1):])

    y_prompt = xp.reshape(n_prompt, tp, D_MODEL)[:, pad + N_META:]
    y_sample = xs.reshape(n_seq, n_new, D_MODEL)
    stk = lambda xs_: jnp.stack(xs_, axis=0)
    return (y_prompt, y_sample, *[stk(o) for o in outs_p], *[stk(o) for o in outs_s])
```

```python
import functools
import math

import jax
import jax.numpy as jnp
from jax import lax
from jax.experimental import pallas as pl
from jax.experimental.pallas import tpu as pltpu

D_MODEL = 1024
N_META = 16
H_A, KV_A, HD_A = 4, 2, 64
G_A = H_A // KV_A
VD_A = 2 * HD_A
H_B, KV_B, HD_B = 8, 2, 64
G_B = H_B // KV_B
H_I, D_I = 8, 64
TOPK_MAX = 256
D_C, CONV_C = 512, 31
D_FF, CONV_F = 2816, 3
ROPE_THETA = 500000.0
ROT_HALF = 8
EPS = 1e-6
LN_EPS = 1e-5

LANE = 128
SUBLANE = 8
ROW_TILE = 256
Q_TILE = 128
KEY_TILE = 256
A_KEY_TILE = 256
FF_CHUNK = 256
VMEM_LIMIT = 56 * 1024 * 1024

LOG2E = 1.4426950408889634
MASKED = -1e30
M_FLOOR = -5e29
NO_SCORE = -3.0e38
INT_MIN = -(2 ** 31)
INT_MAX = 2 ** 31 - 1
KEY_F_MAX = 2147483392.0
SEARCH_CAP = 80

F32 = jnp.float32
BF16 = jnp.bfloat16
I32 = jnp.int32

C_AQ, C_KVA, C_BQ, C_BKK, C_KVB, C_IQ, C_IKK, C_IW, C_CA, C_CG, C_GZ = (
    0, 512, 1024, 1536, 1792, 2048, 2560, 2688, 2816, 3328, 3840)
N_PROJ = C_GZ + 3 * D_MODEL
R_AVT, R_BVT, R_IWT, N_PROJ_T = 0, 256, 512, 528


def _params(sem):
    return pltpu.CompilerParams(dimension_semantics=sem, vmem_limit_bytes=VMEM_LIMIT)


def _const_spec(shape):
    nd = len(shape)
    return pl.BlockSpec(shape, lambda *_: (0,) * nd)


def _rms(x, g):
    return x * lax.rsqrt(jnp.mean(x * x, axis=-1, keepdims=True) + EPS) * g


def _nt_dot(a, b):
    return lax.dot_general(a, b, (((1,), (1,)), ((), ())), preferred_element_type=F32)


def _lane_ids(shape):
    return lax.broadcasted_iota(I32, shape, len(shape) - 1)


def _row_ids(shape):
    return lax.broadcasted_iota(I32, shape, 0)


def _in_proj_kernel(x_ref, g_ref, w_ref, wt_ref, cos_ref, sa_ref, sb_ref, valid_ref,
                    aq_ref, kva_ref, kab_ref, avt_ref, bq_ref, bkk_ref, bvt_ref, kvb_ref,
                    iq_ref, ikk_ref, ikkb_ref, iw_ref, iwt_ref, u_ref, gz_ref):
    xb = _rms(x_ref[...], g_ref[...]).astype(BF16)
    cos, sa, sb = cos_ref[...], sa_ref[...], sb_ref[...]

    def proj(start, width):
        return jnp.dot(xb, w_ref[:, start:start + width], preferred_element_type=F32)

    def proj_t(start, height):
        return _nt_dot(wt_ref[start:start + height, :], xb)

    def rot(z):
        return z * cos + pltpu.roll(z, LANE - ROT_HALF, 1) * sa + pltpu.roll(z, ROT_HALF, 1) * sb

    def groups(z, n_rot):
        n = z.shape[1] // LANE
        return [rot(z[:, k * LANE:(k + 1) * LANE]) if k < n_rot else z[:, k * LANE:(k + 1) * LANE]
                for k in range(n)]

    def put(ref, k, val):
        ref[:, k * LANE:(k + 1) * LANE] = val.astype(ref.dtype)

    for k, zk in enumerate(groups(proj(C_AQ, 512), 4)):
        put(aq_ref, k, zk * (HD_A ** -0.5 * LOG2E))
    for k, zk in enumerate(groups(proj(C_KVA, 512), 2)):
        put(kva_ref, k, zk)
        if k < 2:
            put(kab_ref, k, zk)
    avt = proj_t(R_AVT, 256).astype(BF16)
    for h in range(ROW_TILE // A_KEY_TILE):
        avt_ref[h] = avt[:, h * A_KEY_TILE:(h + 1) * A_KEY_TILE]
    for k, zk in enumerate(groups(proj(C_BQ, 512), 4)):
        put(bq_ref, k, zk * (HD_B ** -0.5 * LOG2E))
    for k, zk in enumerate(groups(proj(C_BKK, 256), 2)):
        put(bkk_ref, k, zk)
    bvt_ref[...] = proj_t(R_BVT, 256).astype(BF16)
    for k, zk in enumerate(groups(proj(C_KVB, 256), 1)):
        put(kvb_ref, k, zk)
    for k, zk in enumerate(groups(proj(C_IQ, 512), 4)):
        put(iq_ref, k, zk * (D_I ** -0.5))
    ikk = rot(proj(C_IKK, LANE))
    ikk_ref[...] = ikk
    ikkb_ref[...] = ikk.astype(BF16)
    iw_ref[...] = proj(C_IW, LANE) * (H_I ** -0.5)
    iwt_ref[...] = proj_t(R_IWT, 2 * SUBLANE)[:H_I] * (H_I ** -0.5)
    u_ref[...] = proj(C_CA, D_C) * jax.nn.sigmoid(proj(C_CG, D_C)) * valid_ref[...]
    gz_ref[...] = proj(C_GZ, 3 * D_MODEL)


IN_PROJ_OUTS = ("aq", "kva", "kab", "avt", "bq", "bkk", "bvt", "kvb", "iq", "ikk", "ikkb", "iw", "iwt", "u", "gz")


def _in_proj(x, g, w, wt, cos, sa, sb, valid, table_tiles):
    n = x.shape[0]
    nt = n // ROW_TILE
    row = lambda width: pl.BlockSpec((ROW_TILE, width), lambda i: (i, 0))
    tab = pl.BlockSpec((ROW_TILE, LANE), lambda i: (i % table_tiles, 0))
    rows = lambda width, dt: (row(width), jax.ShapeDtypeStruct((n, width), dt))
    tile_t = (pl.BlockSpec((None, ROW_TILE, ROW_TILE), lambda i: (i, 0, 0)),
              jax.ShapeDtypeStruct((nt, ROW_TILE, ROW_TILE), BF16))
    n_a = ROW_TILE // A_KEY_TILE
    tile_a = (pl.BlockSpec((None, n_a, ROW_TILE, A_KEY_TILE), lambda i: (i, 0, 0, 0)),
              jax.ShapeDtypeStruct((nt, n_a, ROW_TILE, A_KEY_TILE), BF16))
    outs = [rows(512, BF16), rows(512, F32), rows(256, BF16), tile_a, rows(512, BF16), rows(256, BF16), tile_t,
            rows(256, F32), rows(512, BF16), rows(LANE, F32), rows(LANE, BF16), rows(LANE, F32),
            (pl.BlockSpec((H_I, ROW_TILE), lambda i: (0, i)), jax.ShapeDtypeStruct((H_I, n), F32)),
            rows(D_C, F32), rows(3 * D_MODEL, F32)]
    res = pl.pallas_call(
        _in_proj_kernel,
        grid=(nt,),
        in_specs=[row(D_MODEL), _const_spec((1, D_MODEL)), _const_spec((D_MODEL, N_PROJ)),
                  _const_spec((N_PROJ_T, D_MODEL)),
                  tab, tab, tab, pl.BlockSpec((ROW_TILE, 1), lambda i: (i % table_tiles, 0))],
        out_specs=[o[0] for o in outs],
        out_shape=[o[1] for o in outs],
        compiler_params=_params(("parallel",)),
        name="in_proj",
    )(x, g, w, wt, cos, sa, sb, valid)
    return dict(zip(IN_PROJ_OUTS, res))


def _prep_w_in(w):
    o_aq, o_ak, o_av, o_bq, o_bk, o_bv, o_iq, o_iw, o_ik, o_cin, o_gz = (
        0, 512, 768, 1024, 1536, 1664, 1792, 2304, 2312, 2376, 3400)
    bk0, bk1 = w[:, o_bk:o_bk + 64], w[:, o_bk + 64:o_bk + 128]
    bv0, bv1 = w[:, o_bv:o_bv + 64], w[:, o_bv + 64:o_bv + 128]
    ik = w[:, o_ik:o_ik + D_I]
    iw = w[:, o_iw:o_iw + H_I]
    cols = [w[:, o_aq:o_aq + 512], w[:, o_ak:o_ak + 256], w[:, o_av:o_av + 256], w[:, o_bq:o_bq + 512],
            bk0, bk0, bk1, bk1,
            w[:, o_bk:o_bk + 128], w[:, o_bv:o_bv + 128],
            w[:, o_iq:o_iq + 512], ik, ik,
            iw, jnp.zeros((w.shape[0], LANE - H_I), w.dtype),
            w[:, o_cin:o_cin + 2 * D_C], w[:, o_gz:o_gz + 3 * D_MODEL]]
    out = jnp.concatenate(cols, axis=1).astype(BF16)
    cols_t = [w[:, o_av:o_av + 256], bv0, bv0, bv1, bv1, iw,
              jnp.zeros((w.shape[0], N_PROJ_T - R_IWT - H_I), w.dtype)]
    out_t = jnp.concatenate(cols_t, axis=1).T.astype(BF16)
    assert out.shape[1] == N_PROJ and out_t.shape[0] == N_PROJ_T
    return out, out_t


def _rotary_tables(pos):
    rot = 2 * ROT_HALF
    inv = jnp.power(ROPE_THETA, -jnp.arange(ROT_HALF, dtype=F32) * 2.0 / rot)
    ang = pos.astype(F32)[:, None] * inv[None, :]
    cos, sin = jnp.cos(ang), jnp.sin(ang)
    t = pos.shape[0]
    rest = 64 - rot
    one = jnp.ones((t, rest), F32)
    zero = jnp.zeros((t, rest), F32)
    zh = jnp.zeros((t, ROT_HALF), F32)
    c64 = jnp.concatenate([cos, cos, one], axis=1)
    sa64 = jnp.concatenate([-sin, zh, zero], axis=1)
    sb64 = jnp.concatenate([zh, sin, zero], axis=1)
    dup = lambda a: jnp.concatenate([a, a], axis=1)
    return dup(c64), dup(sa64), dup(sb64)


def _split_pair(pair):
    lo = _lane_ids(pair.shape) < 64
    zero = jnp.zeros_like(pair)
    return jnp.where(lo, pair, zero), jnp.where(lo, zero, pair)


def _monotone_key(score):
    b = lax.bitcast_convert_type(score, I32)
    return jnp.where(b < 0, b ^ I32(0x7FFFFFFF), b)


Q_PAIR = 2 * Q_TILE


def _pair_up(q_groups):
    return [jnp.concatenate(q_groups[n:n + 2], axis=0) for n in range(0, len(q_groups), 2)]


def _two_tile_pipeline(prefetch, consume, state, j_first, j_last, mask_middle):
    n_trips = (j_last - j_first + 2) // 2
    carry = (prefetch(0, j_first, True), state)
    segments = ((0, 1, True), (1, n_trips - 2, mask_middle), (jnp.maximum(n_trips - 2, 1), n_trips, True))
    for first, stop, masked in segments:
        def body(jj, c, masked=masked):
            t0 = j_first + 2 * jj
            aux_b = prefetch(1, t0 + 1, masked)
            st = consume(0, t0, c[0], c[1], masked)
            aux_a = prefetch(0, t0 + 2, masked)
            return aux_a, consume(1, t0 + 1, aux_b, st, masked)
        carry = lax.fori_loop(first, stop, body, carry)
    return carry[1]


def _flash_run_t(load_k, load_vt, bias_of, q_pairs, j_first, j_last, mask_middle, s_ref):
    kv_of = [kv for kv, pairs in enumerate(q_pairs) for _ in pairs]
    qs = [q for pairs in q_pairs for q in pairs]

    def prefetch(slot, j, masked):
        jc = jnp.minimum(j, j_last)
        if masked:
            bias = bias_of(j)
            bias = jnp.concatenate([bias, bias], axis=1)
        k = [load_k(jc, kv) for kv in range(len(q_pairs))]
        cmax = []
        for n, (kv, q) in enumerate(zip(kv_of, qs)):
            s = _nt_dot(k[kv], q)
            if masked:
                s = s + bias
            s_ref[slot, n] = s
            cmax.append(jnp.max(s, axis=0, keepdims=True))
        return cmax

    def step(slot, j, cmax, state, masked):
        del masked
        jc = jnp.minimum(j, j_last)
        v_t = [load_vt(jc, kv) for kv in range(len(q_pairs))]
        out = []
        for n, (kv, (m, l, acc)) in enumerate(zip(kv_of, state)):
            m_new = jnp.maximum(m, cmax[n])
            alpha = jnp.exp2(m - m_new)
            p = jnp.exp2(s_ref[slot, n] - m_new)
            l = alpha * l + jnp.sum(p, axis=0, keepdims=True)
            acc = alpha * acc + jnp.dot(v_t[kv], p.astype(BF16), preferred_element_type=F32)
            out.append((m_new, l, acc))
        return out

    state = [(jnp.full((1, Q_PAIR), M_FLOOR, F32), jnp.zeros((1, Q_PAIR), F32), jnp.zeros((LANE, Q_PAIR), F32))
             for _ in kv_of]
    state = _two_tile_pipeline(prefetch, step, state, j_first, j_last, mask_middle)
    return [acc / jnp.where(l > 0.0, l, 1.0) for _, l, acc in state]


def _attn_a_kernel(lam_ref, q_ref, k_ref, vt_ref, sub_ref, o_ref, s_ref, *, pad, out_scale):
    i = pl.program_id(1)
    q0 = i * Q_TILE

    @pl.when(q0 + Q_TILE <= pad)
    def _():
        o_ref[...] = jnp.zeros_like(o_ref)

    @pl.when(q0 + Q_TILE > pad)
    def _():
        lam = lam_ref[0]
        kt = A_KEY_TILE
        j_pad = pad // kt
        j_diag = (q0 + Q_TILE - 1) // kt
        qpos = _lane_ids((kt, Q_TILE)) + q0
        krow = _row_ids((kt, Q_TILE))
        q_pairs = []
        for kv in range(KV_A):
            q_groups = []
            for g in range(G_A):
                c0 = (kv * G_A + g) * LANE
                q_groups += list(_split_pair(q_ref[:, c0:c0 + LANE]))
            q_pairs.append(_pair_up(q_groups))

        def load_k(j, kv):
            return k_ref[pl.ds(pl.multiple_of(j * kt, kt), kt), kv * LANE:(kv + 1) * LANE]

        def load_vt(j, kv):
            return vt_ref[j, kv * LANE:(kv + 1) * LANE, :]

        def bias_of(j):
            kpos = krow + j * kt
            return jnp.where((kpos <= qpos) & (kpos >= pad), 0.0, MASKED)

        outs = _flash_run_t(load_k, load_vt, bias_of, q_pairs, j_pad, j_diag, False, s_ref)
        for kv in range(KV_A):
            o = outs[kv * G_A:(kv + 1) * G_A]
            for g in range(G_A):
                y = o[g][:, :Q_TILE] - lam * o[g][:, Q_TILE:]
                y = y * lax.rsqrt(jnp.mean(y * y, axis=0, keepdims=True) + EPS) * sub_ref[...] * out_scale
                c0 = (kv * G_A + g) * LANE
                o_ref[:, c0:c0 + LANE] = y.T.astype(o_ref.dtype)


def _attn_a_prompt(lam, aq, kab, avt, subln_col, pad, out_scale):
    b, tp, _ = aq.shape
    return pl.pallas_call(
        functools.partial(_attn_a_kernel, pad=pad, out_scale=out_scale),
        grid=(b, tp // Q_TILE),
        in_specs=[pl.BlockSpec(memory_space=pltpu.SMEM),
                  pl.BlockSpec((None, Q_TILE, 512), lambda bi, i: (bi, i, 0)),
                  pl.BlockSpec((None, tp, 256), lambda bi, i: (bi, 0, 0)),
                  pl.BlockSpec((None, tp // A_KEY_TILE, 256, A_KEY_TILE), lambda bi, i: (bi, 0, 0, 0)),
                  _const_spec((LANE, 1))],
        out_specs=pl.BlockSpec((None, Q_TILE, 512), lambda bi, i: (bi, i, 0)),
        out_shape=jax.ShapeDtypeStruct((b, tp, 512), BF16),
        scratch_shapes=[pltpu.VMEM((2, KV_A * G_A, A_KEY_TILE, Q_PAIR), F32)],
        compiler_params=_params(("parallel", "parallel")),
        name="attn_a_prompt",
    )(lam, aq, kab, avt, subln_col)


def _kth_largest_key(count_ge, k_sel, n_vis, score_min, score_max):
    kf = float(k_sel)
    key_min, key_max = _monotone_key(score_min), _monotone_key(score_max)
    no_key = _monotone_key(jnp.full(n_vis.shape, NO_SCORE, F32))
    zero = jnp.zeros_like(key_min)
    one = zero + 1
    c_pos = count_ge(one)
    c_nn = count_ge(zero)
    few = n_vis <= kf
    pos = c_pos >= kf
    neg = c_nn < kf
    lo = jnp.where(pos, one, jnp.where(neg, key_min, zero))
    hi = jnp.where(pos, key_max + 1, jnp.where(neg, zero, one))
    c_lo = jnp.where(pos, c_pos, jnp.where(neg, n_vis, c_nn))
    c_hi = jnp.where(pos, 0.0, jnp.where(neg, c_nn, c_pos))
    lo = jnp.where(few, no_key + 1, lo)
    done = few | (c_lo == kf) | (hi - 1 <= lo)
    ones = jnp.ones_like(n_vis)

    def cond(st):
        it, active = st[0], st[-1]
        return (it < SEARCH_CAP) & (jnp.sum(active) > 0.0)

    def body(st):
        it, lo, hi, c_lo, c_hi, f_lo, f_hi, last, active = st
        done = active == 0.0
        g_lo = (c_lo - (kf - 0.5)) * f_lo
        g_hi = ((kf - 0.5) - c_hi) * f_hi
        frac = jnp.where(done, 0.5, g_hi / (g_lo + g_hi))
        lo_f, hi_f = lo.astype(F32), hi.astype(F32)
        t_f = jnp.clip(hi_f - (hi_f - lo_f) * frac, -KEY_F_MAX, KEY_F_MAX)
        t_mid = (lo >> 1) + (hi >> 1) + (lo & hi & 1)
        t = jnp.where((f_lo < 0.2) | (f_hi < 0.2), t_mid, t_f.astype(I32))
        t = jnp.minimum(jnp.maximum(t, lo + 1), hi - 1)
        c = count_ge(t)
        up = c >= kf
        act_up = jnp.logical_and(jnp.logical_not(done), up)
        act_dn = jnp.logical_and(jnp.logical_not(done), jnp.logical_not(up))
        f_lo = jnp.where(act_dn & (last == -1), f_lo * 0.5, jnp.where(act_up, ones, f_lo))
        f_hi = jnp.where(act_up & (last == 1), f_hi * 0.5, jnp.where(act_dn, ones, f_hi))
        last = jnp.where(act_up, 1, jnp.where(act_dn, -1, last))
        lo = jnp.where(act_up, t, lo)
        c_lo = jnp.where(act_up, c, c_lo)
        hi = jnp.where(act_dn, t, hi)
        c_hi = jnp.where(act_dn, c, c_hi)
        done = done | (c_lo == kf) | (hi - 1 <= lo)
        return it + 1, lo, hi, c_lo, c_hi, f_lo, f_hi, last, jnp.where(done, 0.0, 1.0)

    st = lax.while_loop(cond, lambda s: body(body(s)),
                        (I32(0), lo, hi, c_lo, c_hi, ones, ones, zero, jnp.where(done, 0.0, 1.0)))
    return st[1], st[3]


def _tie_cut(count_tied_before, n_keep, n_pos_bits):
    def step(it, x):
        cand = x + lax.shift_left(I32(1), I32(n_pos_bits - 1) - it)
        return jnp.where(count_tied_before(cand) < n_keep, cand, x)

    return lax.fori_loop(0, n_pos_bits, step, jnp.zeros(n_keep.shape, I32))


def _attn_b_kernel(iq_ref, iwt_ref, q_ref, ik_ref, kk_ref, vvt_ref, o_ref, key_ref, s_ref, d_ref, *, pad, k_sel):
    i = pl.program_id(1)
    q0 = i * Q_TILE

    @pl.when(q0 + Q_TILE <= pad)
    def _():
        o_ref[...] = jnp.zeros_like(o_ref)

    @pl.when(q0 + Q_TILE > pad)
    def _():
        j_diag = (q0 + Q_TILE - 1) // KEY_TILE
        n_kb = j_diag + 1
        qpos = _lane_ids((KEY_TILE, Q_TILE)) + q0
        krow = _row_ids((KEY_TILE, Q_TILE))
        no_key = _monotone_key(jnp.full((1, 1), NO_SCORE, F32))
        def fold(a, op):
            parts = [a[r * SUBLANE:(r + 1) * SUBLANE] for r in range(KEY_TILE // SUBLANE)]
            while len(parts) > 1:
                parts = [op(parts[n], parts[n + 1]) for n in range(0, len(parts), 2)]
            return parts[0]

        parts = []
        for pr in range(H_I // 2):
            parts += list(_split_pair(iq_ref[:, pr * LANE:(pr + 1) * LANE]))
        qi = jnp.concatenate(parts, axis=0)
        iwt = iwt_ref[...]

        def dots(slot, j, masked):
            k0 = pl.multiple_of(jnp.minimum(j, j_diag) * KEY_TILE, KEY_TILE)
            d_ref[slot] = _nt_dot(ik_ref[pl.ds(k0, KEY_TILE), :], qi)
            return ()

        def score_tile(slot, j, _, carry, masked):
            kmin, kmax = carry
            sc = jnp.zeros((KEY_TILE, Q_TILE), F32)
            for h in range(H_I):
                sc = sc + iwt[h:h + 1, :] * jnp.maximum(d_ref[slot, :, h * Q_TILE:(h + 1) * Q_TILE], 0.0)
            sc = sc + 0.0
            sc_lo = sc_hi = sc
            key = _monotone_key(sc)
            if masked:
                kpos = krow + j * KEY_TILE
                vis = (kpos <= qpos) & (kpos >= pad)
                key = jnp.where(vis, key, no_key)
                sc_lo = jnp.where(vis, sc, -NO_SCORE)
                sc_hi = jnp.where(vis, sc, NO_SCORE)
            key_ref[j] = key
            return jnp.minimum(kmin, fold(sc_lo, jnp.minimum)), jnp.maximum(kmax, fold(sc_hi, jnp.maximum))

        carry = (jnp.full((SUBLANE, Q_TILE), -NO_SCORE, F32), jnp.full((SUBLANE, Q_TILE), NO_SCORE, F32))
        carry = _two_tile_pipeline(dots, score_tile, carry, 0, j_diag, False)
        sc_min = jnp.min(carry[0], axis=0, keepdims=True)
        sc_max = jnp.max(carry[1], axis=0, keepdims=True)

        def count_ge(t):
            def cbody(j, c):
                return c + fold(jnp.where(key_ref[j] >= t, 1.0, 0.0), jnp.add)
            c = lax.fori_loop(0, n_kb, cbody, jnp.zeros((SUBLANE, Q_TILE), F32))
            return jnp.sum(c, axis=0, keepdims=True)

        n_vis = (_lane_ids((1, Q_TILE)) + (q0 - pad + 1)).astype(F32)
        thr, c_thr = _kth_largest_key(count_ge, k_sel, n_vis, sc_min, sc_max)
        tied = c_thr > float(k_sel)

        @pl.when(jnp.sum(jnp.where(tied, 1.0, 0.0)) > 0.0)
        def _():
            n_keep = float(k_sel) - count_ge(thr + 1)

            def tied_before(x):
                def cbody(j, c):
                    hit = (key_ref[j] == thr) & (krow + j * KEY_TILE < x)
                    return c + fold(jnp.where(hit, 1.0, 0.0), jnp.add)
                c = lax.fori_loop(0, n_kb, cbody, jnp.zeros((SUBLANE, Q_TILE), F32))
                return jnp.sum(c, axis=0, keepdims=True)

            x = _tie_cut(tied_before, n_keep, (key_ref.shape[0] * KEY_TILE).bit_length())

            def lower(j, _):
                key = key_ref[j]
                drop = tied & (key == thr) & (krow + j * KEY_TILE > x)
                key_ref[j] = jnp.where(drop, key - 1, key)
                return 0

            lax.fori_loop(0, n_kb, lower, 0)

        q_pairs = []
        for kv in range(KV_B):
            q_groups = []
            for pr in range(G_B // 2):
                c0 = (kv * (G_B // 2) + pr) * LANE
                q_groups += list(_split_pair(q_ref[:, c0:c0 + LANE]))
            q_pairs.append(_pair_up(q_groups))

        def load_k(j, kv):
            return kk_ref[pl.ds(pl.multiple_of(j * KEY_TILE, KEY_TILE), KEY_TILE), kv * LANE:(kv + 1) * LANE]

        def load_vt(j, kv):
            return vvt_ref[j, kv * LANE:(kv + 1) * LANE, :]

        def bias_of(j):
            thr_j = jnp.where(j <= j_diag, thr, INT_MAX)
            return jnp.where(key_ref[jnp.minimum(j, j_diag)] >= thr_j, 0.0, MASKED)

        outs = _flash_run_t(load_k, load_vt, bias_of, q_pairs, 0, j_diag, True, s_ref)
        lo = _lane_ids((Q_TILE, LANE)) < 64
        for kv in range(KV_B):
            o = outs[kv * (G_B // 2):(kv + 1) * (G_B // 2)]
            for pr in range(G_B // 2):
                pair = jnp.where(lo, o[pr][:, :Q_TILE].T, o[pr][:, Q_TILE:].T)
                c0 = (kv * (G_B // 2) + pr) * LANE
                o_ref[:, c0:c0 + LANE] = pair.astype(o_ref.dtype)


def _attn_b_prompt(iq, iwt, bq, ikkb, bkk, bvt, pad, k_sel):
    b, tp, _ = bq.shape
    nq = tp // Q_TILE
    qspec = lambda width: pl.BlockSpec((None, Q_TILE, width), lambda bi, i: (bi, i, 0))
    full = lambda width: pl.BlockSpec((None, tp, width), lambda bi, i: (bi, 0, 0))
    return pl.pallas_call(
        functools.partial(_attn_b_kernel, pad=pad, k_sel=k_sel),
        grid=(b, nq),
        in_specs=[qspec(512), pl.BlockSpec((H_I, Q_TILE), lambda bi, i: (0, bi * nq + i)), qspec(512),
                  full(LANE), full(256),
                  pl.BlockSpec((None, tp // KEY_TILE, 256, KEY_TILE), lambda bi, i: (bi, 0, 0, 0))],
        out_specs=qspec(512),
        out_shape=jax.ShapeDtypeStruct((b, tp, 512), BF16),
        scratch_shapes=[pltpu.VMEM((tp // KEY_TILE + 1, KEY_TILE, Q_TILE), I32),
                        pltpu.VMEM((2, KV_B * G_B // 2, KEY_TILE, Q_PAIR), F32),
                        pltpu.VMEM((2, KEY_TILE, H_I * Q_TILE), F32)],
        compiler_params=_params(("parallel", "parallel")),
        name="attn_b_prompt",
    )(iq, iwt, bq, ikkb, bkk, bvt)


def _ln_swish(y, g, b):
    mu = jnp.mean(y, axis=-1, keepdims=True)
    yc = y - mu
    z = yc * lax.rsqrt(jnp.mean(yc * yc, axis=-1, keepdims=True) + LN_EPS) * g + b
    return z * jax.nn.sigmoid(z)


HALO = 32


def _conv_c_prompt_kernel(prev_ref, cur_ref, w_ref, b_ref, g_ref, beta_ref, o_ref, ext_ref):
    ext_ref[0:HALO, :] = prev_ref[ROW_TILE - HALO:ROW_TILE, :]
    ext_ref[HALO:HALO + ROW_TILE, :] = cur_ref[...]
    acc = jnp.zeros((ROW_TILE, D_C), F32) + b_ref[...]
    base = HALO - (CONV_C - 1)
    for j in range(CONV_C):
        acc = acc + w_ref[j:j + 1, :] * ext_ref[base + j:base + j + ROW_TILE, :]
    o_ref[...] = _ln_swish(acc, g_ref[...], beta_ref[...]).astype(o_ref.dtype)


def _conv_c_prompt(u, w, b, g, beta):
    n = u.shape[0]
    return pl.pallas_call(
        _conv_c_prompt_kernel,
        grid=(n // ROW_TILE,),
        in_specs=[pl.BlockSpec((ROW_TILE, D_C), lambda i: (jnp.maximum(i - 1, 0), 0)),
                  pl.BlockSpec((ROW_TILE, D_C), lambda i: (i, 0)),
                  _const_spec((CONV_C, D_C)), _const_spec((1, D_C)), _const_spec((1, D_C)),
                  _const_spec((1, D_C))],
        out_specs=pl.BlockSpec((ROW_TILE, D_C), lambda i: (i, 0)),
        out_shape=jax.ShapeDtypeStruct((n, D_C), BF16),
        scratch_shapes=[pltpu.VMEM((HALO + ROW_TILE, D_C), F32)],
        compiler_params=_params(("parallel",)),
        name="conv_c_prompt",
    )(u, u, w, b, g, beta)


def _conv_c_sample_kernel(st_ref, u_ref, w_ref, b_ref, g_ref, beta_ref, o_ref):
    n_hist = st_ref.shape[0]
    for t in range(u_ref.shape[0]):
        acc = jnp.zeros(o_ref.shape[1:], F32) + b_ref[...]
        for j in range(CONV_C):
            e = t + j
            row = st_ref[e] if e < n_hist else u_ref[e - n_hist]
            acc = acc + w_ref[j:j + 1, :] * row
        o_ref[t] = _ln_swish(acc, g_ref[...], beta_ref[...]).astype(o_ref.dtype)


def _conv_c_sample(st_t, u_t, w, b, g, beta):
    n_new, n_seq, _ = u_t.shape
    return pl.pallas_call(
        _conv_c_sample_kernel,
        grid=(1,),
        in_specs=[_const_spec(st_t.shape), _const_spec(u_t.shape), _const_spec((CONV_C, D_C)),
                  _const_spec((1, D_C)), _const_spec((1, D_C)), _const_spec((1, D_C))],
        out_specs=_const_spec((n_new, n_seq, D_C)),
        out_shape=jax.ShapeDtypeStruct((n_new, n_seq, D_C), BF16),
        compiler_params=_params(("arbitrary",)),
        name="conv_c_sample",
    )(st_t, u_t, w, b, g, beta)


def _merge_kernel(ya_ref, yb_ref, yc_ref, gz_ref, x_ref, wa_ref, wb_ref, wc_ref, wo_ref, o_ref):
    merged = jnp.zeros((ROW_TILE, D_MODEL), F32)
    for k, (y_ref, w_ref) in enumerate(((ya_ref, wa_ref), (yb_ref, wb_ref), (yc_ref, wc_ref))):
        gate = jax.nn.sigmoid(gz_ref[:, k * D_MODEL:(k + 1) * D_MODEL])
        merged = merged + gate * jnp.dot(y_ref[...], w_ref[...], preferred_element_type=F32)
    o_ref[...] = x_ref[...] + jnp.dot(merged.astype(BF16), wo_ref[...], preferred_element_type=F32)


def _merge(ya, yb, yc, gz, x, wa, wb, wc, wo):
    n = x.shape[0]
    row = lambda width: pl.BlockSpec((ROW_TILE, width), lambda i: (i, 0))
    return pl.pallas_call(
        _merge_kernel,
        grid=(n // ROW_TILE,),
        in_specs=[row(512), row(512), row(D_C), row(3 * D_MODEL), row(D_MODEL),
                  _const_spec(wa.shape), _const_spec(wb.shape), _const_spec(wc.shape), _const_spec(wo.shape)],
        out_specs=row(D_MODEL),
        out_shape=jax.ShapeDtypeStruct((n, D_MODEL), F32),
        compiler_params=_params(("parallel",)),
        name="merge",
    )(ya, yb, yc, gz, x, wa, wb, wc, wo)


TAIL = 8


def _ffn_kernel(*refs, carried, final_norm, n_new):
    if carried:
        (x_ref, g_ref, wu_ref, cw_ref, cb_ref, wd_ref, gf_ref, valid_ref,
         o_ref, tail_ref, carry_ref) = refs
    else:
        (x_ref, g_ref, wu_ref, cw_ref, cb_ref, wd_ref, gf_ref, h1_ref, h2_ref,
         o_ref, up_ref) = refs
    x = x_ref[...]
    xb = _rms(x, g_ref[...]).astype(BF16)
    row = _row_ids((ROW_TILE, 1))

    if carried:
        @pl.when(pl.program_id(0) == 0)
        def _():
            carry_ref[...] = jnp.zeros_like(carry_ref)
        valid = valid_ref[...]
    else:
        t_in_seq = row % n_new

    def conv(c0):
        up = jnp.dot(xb, wu_ref[:, c0:c0 + FF_CHUNK], preferred_element_type=F32)
        if carried:
            up = up * valid
            old = carry_ref[:, c0:c0 + FF_CHUNK]
            p1 = jnp.where(row == 0, old[TAIL - 1:TAIL], pltpu.roll(up, 1, 0))
            p2 = jnp.where(row == 0, old[TAIL - 2:TAIL - 1],
                           jnp.where(row == 1, old[TAIL - 1:TAIL], pltpu.roll(up, 2, 0)))
            carry_ref[:, c0:c0 + FF_CHUNK] = up[ROW_TILE - TAIL:]
            tail_ref[:, c0:c0 + FF_CHUNK] = up[ROW_TILE - TAIL:]
        else:
            p1 = jnp.where(t_in_seq < 1, h1_ref[:, c0:c0 + FF_CHUNK], pltpu.roll(up, 1, 0))
            p2 = jnp.where(t_in_seq < 2, h2_ref[:, c0:c0 + FF_CHUNK], pltpu.roll(up, 2, 0))
            up_ref[:, c0:c0 + FF_CHUNK] = up
        w = cw_ref[:, c0:c0 + FF_CHUNK]
        return w[0:1] * p2 + w[1:2] * p1 + w[2:3] * up + cb_ref[:, c0:c0 + FF_CHUNK]

    acc = jnp.zeros((ROW_TILE, D_MODEL), F32)
    for c in range(D_FF // FF_CHUNK):
        hg = conv(c * FF_CHUNK)
        hv = conv(D_FF + c * FF_CHUNK)
        h = (hg * jax.nn.sigmoid(hg) * hv).astype(BF16)
        acc = acc + jnp.dot(h, wd_ref[c * FF_CHUNK:(c + 1) * FF_CHUNK, :], preferred_element_type=F32)
    y = x + acc
    o_ref[...] = _rms(y, gf_ref[...]) if final_norm else y


def _ffn(x, g, wu, cw, cb, wd, gf, extra, carried, final_norm, n_new=1):
    n = x.shape[0]
    nt = n // ROW_TILE
    row = lambda width: pl.BlockSpec((ROW_TILE, width), lambda i: (i, 0))
    in_specs = [row(D_MODEL), _const_spec((1, D_MODEL)), _const_spec(wu.shape), _const_spec(cw.shape),
                _const_spec(cb.shape), _const_spec(wd.shape), _const_spec((1, D_MODEL))]
    if carried:
        in_specs += [pl.BlockSpec((ROW_TILE, 1), lambda i: (i, 0))]
        out_specs = [row(D_MODEL), pl.BlockSpec((None, TAIL, 2 * D_FF), lambda i: (i, 0, 0))]
        out_shape = [jax.ShapeDtypeStruct((n, D_MODEL), F32), jax.ShapeDtypeStruct((nt, TAIL, 2 * D_FF), F32)]
        scratch = [pltpu.VMEM((TAIL, 2 * D_FF), F32)]
    else:
        in_specs += [row(2 * D_FF), row(2 * D_FF)]
        out_specs = [row(D_MODEL), row(2 * D_FF)]
        out_shape = [jax.ShapeDtypeStruct((n, D_MODEL), F32), jax.ShapeDtypeStruct((n, 2 * D_FF), F32)]
        scratch = []
    return pl.pallas_call(
        functools.partial(_ffn_kernel, carried=carried, final_norm=final_norm, n_new=n_new),
        grid=(nt,),
        in_specs=in_specs, out_specs=out_specs, out_shape=out_shape, scratch_shapes=scratch,
        compiler_params=_params(("arbitrary",)),
        name="ffn_prompt" if carried else "ffn_sample",
    )(x, g, wu, cw, cb, wd, gf, *extra)


S_ROWS = 8


def _new_key_mask(shape, n_past, n_new):
    col = _lane_ids(shape)
    t = _row_ids(shape) % S_ROWS
    return (col < n_past) | ((col - n_past <= t) & (col < n_past + n_new))


def _pad_rows(a, rows):
    return jnp.concatenate([a, jnp.zeros((rows - a.shape[0], a.shape[1]), a.dtype)], axis=0)


def _page_specs(block, layer, n_pages):
    nd = len(block)
    return [pl.BlockSpec((None, None) + block,
                         lambda s, pt, j=j: (layer, pt[s * n_pages + j]) + (0,) * nd) for j in range(n_pages)]


def _attn_a_sample_kernel(pt_ref, lam_ref, q_ref, new_ref, sub_ref, *rest, n_pages, page, n_new, out_scale):
    page_refs, o_ref = rest[:n_pages], rest[n_pages]
    lam = lam_ref[0]
    n_past = n_pages * page
    q = q_ref[...].astype(F32)
    new = _pad_rows(new_ref[...], LANE)
    for kv in range(KV_A):
        parts = []
        for g in range(G_A):
            c0 = (kv * G_A + g) * LANE
            parts += list(_split_pair(q[:, c0:c0 + LANE]))
        qs = jnp.concatenate(parts, axis=0).astype(BF16)
        ksl = slice(kv * LANE, (kv + 1) * LANE)
        vsl = slice((KV_A + kv) * LANE, (KV_A + kv + 1) * LANE)
        k = jnp.concatenate([r[pl.ds(kv, page, stride=2 * KV_A), :].astype(BF16) for r in page_refs]
                            + [new[:, ksl].astype(BF16)], axis=0)
        v = jnp.concatenate([r[pl.ds(KV_A + kv, page, stride=2 * KV_A), :].astype(BF16) for r in page_refs]
                            + [new[:, vsl].astype(BF16)], axis=0)
        s = _nt_dot(qs, k)
        s = jnp.where(_new_key_mask(s.shape, n_past, n_new), s, MASKED)
        p = jnp.exp2(s - jnp.max(s, axis=1, keepdims=True))
        o = jnp.dot(p.astype(BF16), v, preferred_element_type=F32) / jnp.sum(p, axis=1, keepdims=True)
        for g in range(G_A):
            b0 = g * 2 * S_ROWS
            y = o[b0:b0 + S_ROWS] - lam * o[b0 + S_ROWS:b0 + 2 * S_ROWS]
            y = _rms(y, sub_ref[...]) * out_scale
            c0 = (kv * G_A + g) * LANE
            o_ref[:, c0:c0 + LANE] = y.astype(o_ref.dtype)


def _attn_a_sample(pt, lam, aq, new, subln, cache, layer, n_new, out_scale):
    n_seq = aq.shape[0]
    n_pages = pt.shape[0] // n_seq
    page = cache.shape[2] // (2 * KV_A)
    seq = lambda width: pl.BlockSpec((None, S_ROWS, width), lambda s, pt: (s, 0, 0))
    grid_spec = pltpu.PrefetchScalarGridSpec(
        num_scalar_prefetch=1,
        grid=(n_seq,),
        in_specs=[pl.BlockSpec(memory_space=pltpu.SMEM), seq(512), seq(512),
                  pl.BlockSpec((1, LANE), lambda s, pt: (0, 0))]
        + _page_specs((page * 2 * KV_A, LANE), layer, n_pages),
        out_specs=seq(512),
    )
    return pl.pallas_call(
        functools.partial(_attn_a_sample_kernel, n_pages=n_pages, page=page, n_new=n_new, out_scale=out_scale),
        grid_spec=grid_spec,
        out_shape=jax.ShapeDtypeStruct((n_seq, S_ROWS, 512), BF16),
        compiler_params=_params(("parallel",)),
        name="attn_a_sample",
    )(pt, lam, aq, new, subln, *([cache] * n_pages))


def _attn_b_sample_kernel(pt_ref, iq_ref, iw_ref, q_ref, newkv_ref, newik_ref, *rest,
                          n_pages, page, n_new, k_sel):
    kv_refs, ik_refs, o_ref = rest[:n_pages], rest[n_pages:2 * n_pages], rest[2 * n_pages]
    n_past = n_pages * page
    lo = _lane_ids((S_ROWS, LANE)) < 64

    def heads64(x):
        out = []
        for pr in range(x.shape[1] // LANE):
            pair = x[:, pr * LANE:(pr + 1) * LANE]
            out += [pair[:, :64], pltpu.roll(pair, 64, 1)[:, :64]]
        return out

    qi = jnp.concatenate(heads64(iq_ref[...].astype(F32)), axis=0).astype(BF16)
    ik_t = jnp.concatenate([r[...].astype(BF16) for r in ik_refs], axis=1)
    newik = _pad_rows(newik_ref[:, :D_I], LANE).astype(BF16)
    d = jnp.concatenate([jnp.dot(qi, ik_t, preferred_element_type=F32), _nt_dot(qi, newik)], axis=1)
    d = jnp.maximum(d, 0.0)
    iw = iw_ref[...]
    sc = jnp.zeros((S_ROWS, d.shape[1]), F32)
    for h in range(H_I):
        sc = sc + iw[:, h:h + 1] * d[h * S_ROWS:(h + 1) * S_ROWS]
    sc = sc + 0.0
    vis = _new_key_mask(sc.shape, n_past, n_new)
    no_key = _monotone_key(jnp.full((1, 1), NO_SCORE, F32))
    keys = jnp.where(vis, _monotone_key(sc), no_key)

    def count_ge(t):
        return jnp.sum(jnp.where(keys >= t, 1.0, 0.0), axis=1, keepdims=True)

    n_vis = jnp.sum(jnp.where(vis, 1.0, 0.0), axis=1, keepdims=True)
    sc_min = jnp.min(jnp.where(vis, sc, -NO_SCORE), axis=1, keepdims=True)
    sc_max = jnp.max(jnp.where(vis, sc, NO_SCORE), axis=1, keepdims=True)
    thr, c_thr = _kth_largest_key(count_ge, k_sel, n_vis, sc_min, sc_max)
    tied = c_thr > float(k_sel)

    def break_ties(keys):
        n_keep = float(k_sel) - count_ge(thr + 1)
        kpos = _lane_ids(keys.shape)

        def tied_before(x):
            return jnp.sum(jnp.where((keys == thr) & (kpos < x), 1.0, 0.0), axis=1, keepdims=True)

        x = _tie_cut(tied_before, n_keep, keys.shape[1].bit_length())
        return jnp.where(tied & (keys == thr) & (kpos > x), keys - 1, keys)

    keys = lax.cond(jnp.sum(jnp.where(tied, 1.0, 0.0)) > 0.0, break_ties, lambda a: a, keys)
    bias = jnp.concatenate([jnp.where(keys >= thr, 0.0, MASKED)] * G_B, axis=0)

    q_heads = heads64(q_ref[...].astype(F32))
    newkv = _pad_rows(newkv_ref[...], LANE)
    newk128, newv128 = newkv[:, 0:LANE], newkv[:, LANE:2 * LANE]
    for kv in range(KV_B):
        qs = jnp.concatenate(q_heads[kv * G_B:(kv + 1) * G_B], axis=0).astype(BF16)
        k_t = jnp.concatenate([r[kv * 64:(kv + 1) * 64, :].astype(BF16) for r in kv_refs], axis=1)
        v_t = jnp.concatenate([r[(KV_B + kv) * 64:(KV_B + kv + 1) * 64, :].astype(BF16) for r in kv_refs],
                              axis=1)
        v_t2 = jnp.concatenate([v_t, v_t], axis=0)
        newk = (newk128 if kv == 0 else pltpu.roll(newk128, 64, 1))[:, :64].astype(BF16)
        s = jnp.concatenate([jnp.dot(qs, k_t, preferred_element_type=F32), _nt_dot(qs, newk)], axis=1) + bias
        m = jnp.maximum(jnp.max(s, axis=1, keepdims=True), M_FLOOR)
        p = jnp.exp2(s - m)
        l = jnp.sum(p, axis=1, keepdims=True)
        pb = p.astype(BF16)
        o_new = jnp.dot(pb[:, n_past:], newv128.astype(BF16), preferred_element_type=F32)
        swapped = pltpu.roll(o_new, 64, 1)
        lo4 = jnp.concatenate([lo] * G_B, axis=0)
        o_new = jnp.where(lo4, o_new, swapped) if kv == 0 else jnp.where(lo4, swapped, o_new)
        o = (_nt_dot(pb[:, :n_past], v_t2) + o_new) / jnp.where(l > 0.0, l, 1.0)
        for pr in range(G_B // 2):
            even = o[2 * pr * S_ROWS:(2 * pr + 1) * S_ROWS]
            odd = o[(2 * pr + 1) * S_ROWS:(2 * pr + 2) * S_ROWS]
            c0 = (kv * (G_B // 2) + pr) * LANE
            o_ref[:, c0:c0 + LANE] = jnp.where(lo, even, odd).astype(o_ref.dtype)


def _attn_b_sample(pt, iq, iw, bq, newkv, newik, cache_kv_t, cache_ik_t, layer, n_new, k_sel):
    n_seq = bq.shape[0]
    n_pages = pt.shape[0] // n_seq
    page = cache_kv_t.shape[3]
    seq = lambda width: pl.BlockSpec((None, S_ROWS, width), lambda s, pt: (s, 0, 0))
    grid_spec = pltpu.PrefetchScalarGridSpec(
        num_scalar_prefetch=1,
        grid=(n_seq,),
        in_specs=[seq(512), seq(LANE), seq(512), seq(256), seq(LANE)]
        + _page_specs((2 * KV_B * HD_B, page), layer, n_pages) + _page_specs((D_I, page), layer, n_pages),
        out_specs=seq(512),
    )
    return pl.pallas_call(
        functools.partial(_attn_b_sample_kernel, n_pages=n_pages, page=page, n_new=n_new, k_sel=k_sel),
        grid_spec=grid_spec,
        out_shape=jax.ShapeDtypeStruct((n_seq, S_ROWS, 512), BF16),
        compiler_params=_params(("parallel",)),
        name="attn_b_sample",
    )(pt, iq, iw, bq, newkv, newik, *([cache_kv_t] * n_pages), *([cache_ik_t] * n_pages))


def kernel(x_prompt, x_sample, cache_kv_a, cache_kv_b, cache_idx_b, state_conv_c, state_conv_ffn, page_table, meta_tokens, norm_mix, w_in, lam_q1, lam_k1, lam_q2, lam_k2, subln_a, w_out_a, w_out_b, conv_c_w, conv_c_b, ln_c_g, ln_c_b, w_out_c, w_o, norm_ffn, w_up, conv_f_w, conv_f_b, w_down, norm_final):
    depth = w_in.shape[0]
    n_prompt, seq, _ = x_prompt.shape
    t_len = seq + N_META
    pad = (-t_len) % ROW_TILE
    tp = t_len + pad
    n_seq, n_new, _ = x_sample.shape
    n_pool, page = cache_kv_a.shape[1:3]
    n_pages = page_table.shape[1]
    past_len = n_pages * page
    k_sel_p = min(TOPK_MAX, seq // 4)
    k_sel_s = min(TOPK_MAX, (past_len + n_new) // 4)
    assert n_new <= S_ROWS and ROW_TILE % n_new == 0 and (n_seq * n_new) % ROW_TILE == 0
    assert tp % ROW_TILE == 0 and KEY_TILE == ROW_TILE and page == LANE

    xp = jnp.concatenate([jnp.zeros((n_prompt, pad, D_MODEL), F32),
                          jnp.broadcast_to(meta_tokens[None], (n_prompt, N_META, D_MODEL)), x_prompt], axis=1)
    xp = xp.reshape(n_prompt * tp, D_MODEL)
    xs = x_sample.reshape(n_seq * n_new, D_MODEL)
    tab_p = _rotary_tables(jnp.arange(tp, dtype=I32) - pad)
    tab_s = _rotary_tables(past_len + (jnp.arange(ROW_TILE, dtype=I32) % n_new))
    valid_p = (jnp.arange(tp) >= pad).astype(F32)[:, None]
    valid_s = jnp.ones((ROW_TILE, 1), F32)
    valid_p_flat = jnp.tile(valid_p, (n_prompt, 1))
    pt_flat = page_table.reshape(-1).astype(I32)
    cache_a = cache_kv_a.reshape(depth, n_pool, page * 2 * KV_A, 2 * HD_A)
    cache_b_t = jnp.transpose(cache_kv_b, (0, 1, 3, 4, 5, 2)).reshape(depth, n_pool, 2 * KV_B * HD_B, page)
    cache_i_t = jnp.transpose(cache_idx_b, (0, 1, 3, 2))

    row1 = lambda a: a.reshape(1, -1)
    pad_seq = lambda a: jnp.pad(a.reshape(n_seq, n_new, -1), ((0, 0), (0, S_ROWS - n_new), (0, 0)))
    unpad_seq = lambda a: a[:, :n_new].reshape(n_seq * n_new, -1)
    per_seq = lambda a: a.reshape(n_prompt, tp, a.shape[-1])
    tiles_t = lambda a: a.reshape(n_prompt, tp // KEY_TILE, ROW_TILE, KEY_TILE)

    outs_p = [[] for _ in range(5)]
    outs_s = [[] for _ in range(5)]
    for l in range(depth):
        lam_init = 0.8 - 0.6 * math.exp(-0.3 * l)
        lam = (jnp.exp(jnp.sum(lam_q1[l] * lam_k1[l])) - jnp.exp(jnp.sum(lam_q2[l] * lam_k2[l]))
               + lam_init).reshape(1).astype(F32)
        out_scale = 1.0 - lam_init
        last = l == depth - 1
        w_in_l, w_in_t = _prep_w_in(w_in[l])
        wa, wb, wc, wo = (w_out_a[l].astype(BF16), w_out_b[l].astype(BF16), w_out_c[l].astype(BF16),
                          w_o[l].astype(BF16))
        wu, wd = w_up[l].astype(BF16), w_down[l].astype(BF16)
        ffn_w = (row1(norm_ffn[l]), wu, conv_f_w[l], row1(conv_f_b[l]), wd, row1(norm_final))
        conv_w = (conv_c_w[l], row1(conv_c_b[l]), row1(ln_c_g[l]), row1(ln_c_b[l]))

        z = _in_proj(xp, row1(norm_mix[l]), w_in_l, w_in_t, *tab_p, valid_p, tp // ROW_TILE)
        ya = _attn_a_prompt(lam, per_seq(z["aq"]), per_seq(z["kab"]),
                            z["avt"].reshape(n_prompt, tp // A_KEY_TILE, ROW_TILE, A_KEY_TILE),
                            subln_a[l].reshape(-1, 1), pad, out_scale)
        yb = _attn_b_prompt(per_seq(z["iq"]), z["iwt"], per_seq(z["bq"]), per_seq(z["ikkb"]), per_seq(z["bkk"]),
                            tiles_t(z["bvt"]), pad, k_sel_p)
        yc = _conv_c_prompt(z["u"], *conv_w)
        xp = _merge(ya.reshape(-1, 512), yb.reshape(-1, 512), yc, z["gz"], xp, wa, wb, wc, wo)
        xp, tails = _ffn(xp, *ffn_w, (valid_p_flat,), carried=True, final_norm=last)
        outs_p[0].append(per_seq(z["kva"])[:, pad:].reshape(n_prompt, t_len, 2, KV_A, 2 * HD_A))
        outs_p[1].append(per_seq(z["kvb"])[:, pad:].reshape(n_prompt, t_len, 2, KV_B, HD_B))
        outs_p[2].append(per_seq(z["ikk"])[:, pad:, :D_I])
        outs_p[3].append(per_seq(z["u"])[:, tp - (CONV_C - 1):])
        tails = tails.reshape(n_prompt, tp // ROW_TILE, TAIL, 2 * D_FF)
        outs_p[4].append(tails[:, -1, TAIL - (CONV_F - 1):])

        z = _in_proj(xs, row1(norm_mix[l]), w_in_l, w_in_t, *tab_s, valid_s, 1)
        ya = _attn_a_sample(pt_flat, lam, pad_seq(z["aq"]), pad_seq(z["kva"]), row1(subln_a[l]), cache_a, l,
                            n_new, out_scale)
        yb = _attn_b_sample(pt_flat, pad_seq(z["iq"]), pad_seq(z["iw"]), pad_seq(z["bq"]), pad_seq(z["kvb"]),
                            pad_seq(z["ikk"]), cache_b_t, cache_i_t, l, n_new, k_sel_s)
        u_seq = z["u"].reshape(n_seq, n_new, D_C)
        yc = _conv_c_sample(jnp.swapaxes(state_conv_c[l], 0, 1), jnp.swapaxes(u_seq, 0, 1), *conv_w)
        yc = jnp.swapaxes(yc, 0, 1).reshape(n_seq * n_new, D_C)
        xs = _merge(unpad_seq(ya), unpad_seq(yb), yc, z["gz"], xs, wa, wb, wc, wo)
        st_f = state_conv_ffn[l]
        zrow = jnp.zeros_like(st_f[:, :1])
        h1 = jnp.concatenate([st_f[:, 1:2]] + [zrow] * (n_new - 1), axis=1).reshape(n_seq * n_new, 2 * D_FF)
        h2 = jnp.concatenate([st_f[:, 0:1], st_f[:, 1:2]] + [zrow] * (n_new - 2), axis=1).reshape(
            n_seq * n_new, 2 * D_FF)
        xs, up = _ffn(xs, *ffn_w, (h1, h2), carried=False, final_norm=last, n_new=n_new)
        outs_s[0].append(z["kva"].reshape(n_seq, n_new, 2, KV_A, 2 * HD_A))
        outs_s[1].append(z["kvb"].reshape(n_seq, n_new, 2, KV_B, HD_B))
        outs_s[2].append(z["ikk"][:, :D_I].reshape(n_seq, n_new, D_I))
        outs_s[3].append(jnp.concatenate([state_conv_c[l], u_seq], axis=1)[:, -(CONV_C - 1):])
        up_ext = jnp.concatenate([st_f, up.reshape(n_seq, n_new, 2 * D_FF)], axis=1)
        outs_s[4].append(up_ext[:, -(CONV_F - 1):])

    y_prompt = xp.reshape(n_prompt, tp, D_MODEL)[:, pad + N_META:]
    y_sample = xs.reshape(n_seq, n_new, D_MODEL)
    stk = lambda xs_: jnp.stack(xs_, axis=0)
    return (y_prompt, y_sample, *[stk(o) for o in outs_p], *[stk(o) for o in outs_s])
```

```python
import functools
import math

import jax
import jax.numpy as jnp
from jax import lax
from jax.experimental import pallas as pl
from jax.experimental.pallas import tpu as pltpu

D_MODEL = 1024
N_META = 16
H_A, KV_A, HD_A = 4, 2, 64
G_A = H_A // KV_A
VD_A = 2 * HD_A
H_B, KV_B, HD_B = 8, 2, 64
G_B = H_B // KV_B
H_I, D_I = 8, 64
TOPK_MAX = 256
D_C, CONV_C = 512, 31
D_FF, CONV_F = 2816, 3
ROPE_THETA = 500000.0
ROT_HALF = 8
EPS = 1e-6
LN_EPS = 1e-5

LANE = 128
SUBLANE = 8
ROW_TILE = 256
Q_TILE = 128
KEY_TILE = 256
A_KEY_TILE = 256
FF_CHUNK = 256
FFN_TILE = 512
VMEM_LIMIT = 56 * 1024 * 1024

LOG2E = 1.4426950408889634
MASKED = -1e30
M_FLOOR = -5e29
NO_SCORE = -3.0e38
INT_MIN = -(2 ** 31)
INT_MAX = 2 ** 31 - 1
KEY_F_MAX = 2147483392.0
SEARCH_CAP = 80

F32 = jnp.float32
BF16 = jnp.bfloat16
I32 = jnp.int32

C_AQ, C_KVA, C_BQ, C_BKK, C_KVB, C_IQ, C_IKK, C_IW, C_CA, C_CG, C_GZ = (
    0, 512, 1024, 1536, 1792, 2048, 2560, 2688, 2816, 3328, 3840)
N_PROJ = C_GZ + 3 * D_MODEL
R_AVT, R_BVT, R_IWT, N_PROJ_T = 0, 256, 512, 528


def _params(sem):
    return pltpu.CompilerParams(dimension_semantics=sem, vmem_limit_bytes=VMEM_LIMIT)


def _const_spec(shape):
    nd = len(shape)
    return pl.BlockSpec(shape, lambda *_: (0,) * nd)


def _rms(x, g):
    return x * lax.rsqrt(jnp.mean(x * x, axis=-1, keepdims=True) + EPS) * g


def _nt_dot(a, b):
    return lax.dot_general(a, b, (((1,), (1,)), ((), ())), preferred_element_type=F32)


def _lane_ids(shape):
    return lax.broadcasted_iota(I32, shape, len(shape) - 1)


def _row_ids(shape):
    return lax.broadcasted_iota(I32, shape, 0)


def _in_proj_kernel(x_ref, g_ref, w_ref, wt_ref, cos_ref, sa_ref, sb_ref, valid_ref,
                    aq_ref, kva_ref, kab_ref, avt_ref, bq_ref, bkk_ref, bvt_ref, kvb_ref,
                    iq_ref, ikk_ref, ikkb_ref, iw_ref, iwt_ref, u_ref, gz_ref):
    xb = _rms(x_ref[...], g_ref[...]).astype(BF16)
    cos, sa, sb = cos_ref[...], sa_ref[...], sb_ref[...]

    def proj(start, width):
        return jnp.dot(xb, w_ref[:, start:start + width], preferred_element_type=F32)

    def proj_t(start, height):
        return _nt_dot(wt_ref[start:start + height, :], xb)

    def rot(z):
        return z * cos + pltpu.roll(z, LANE - ROT_HALF, 1) * sa + pltpu.roll(z, ROT_HALF, 1) * sb

    def groups(z, n_rot):
        n = z.shape[1] // LANE
        return [rot(z[:, k * LANE:(k + 1) * LANE]) if k < n_rot else z[:, k * LANE:(k + 1) * LANE]
                for k in range(n)]

    def put(ref, k, val):
        ref[:, k * LANE:(k + 1) * LANE] = val.astype(ref.dtype)

    for k, zk in enumerate(groups(proj(C_AQ, 512), 4)):
        put(aq_ref, k, zk * (HD_A ** -0.5 * LOG2E))
    for k, zk in enumerate(groups(proj(C_KVA, 512), 2)):
        put(kva_ref, k, zk)
        if k < 2:
            put(kab_ref, k, zk)
    avt = proj_t(R_AVT, 256).astype(BF16)
    for h in range(ROW_TILE // A_KEY_TILE):
        avt_ref[h] = avt[:, h * A_KEY_TILE:(h + 1) * A_KEY_TILE]
    for k, zk in enumerate(groups(proj(C_BQ, 512), 4)):
        put(bq_ref, k, zk * (HD_B ** -0.5 * LOG2E))
    for k, zk in enumerate(groups(proj(C_BKK, 256), 2)):
        put(bkk_ref, k, zk)
    bvt_ref[...] = proj_t(R_BVT, 256).astype(BF16)
    for k, zk in enumerate(groups(proj(C_KVB, 256), 1)):
        put(kvb_ref, k, zk)
    for k, zk in enumerate(groups(proj(C_IQ, 512), 4)):
        put(iq_ref, k, zk * (D_I ** -0.5))
    ikk = rot(proj(C_IKK, LANE))
    ikk_ref[...] = ikk
    ikkb_ref[...] = ikk.astype(BF16)
    iw_ref[...] = proj(C_IW, LANE) * (H_I ** -0.5)
    iwt_ref[...] = proj_t(R_IWT, 2 * SUBLANE)[:H_I] * (H_I ** -0.5)
    u_ref[...] = proj(C_CA, D_C) * jax.nn.sigmoid(proj(C_CG, D_C)) * valid_ref[...]
    gz_ref[...] = proj(C_GZ, 3 * D_MODEL)


IN_PROJ_OUTS = ("aq", "kva", "kab", "avt", "bq", "bkk", "bvt", "kvb", "iq", "ikk", "ikkb", "iw", "iwt", "u", "gz")


def _in_proj(x, g, w, wt, cos, sa, sb, valid, table_tiles):
    n = x.shape[0]
    nt = n // ROW_TILE
    row = lambda width: pl.BlockSpec((ROW_TILE, width), lambda i: (i, 0))
    tab = pl.BlockSpec((ROW_TILE, LANE), lambda i: (i % table_tiles, 0))
    rows = lambda width, dt: (row(width), jax.ShapeDtypeStruct((n, width), dt))
    tile_t = (pl.BlockSpec((None, ROW_TILE, ROW_TILE), lambda i: (i, 0, 0)),
              jax.ShapeDtypeStruct((nt, ROW_TILE, ROW_TILE), BF16))
    n_a = ROW_TILE // A_KEY_TILE
    tile_a = (pl.BlockSpec((None, n_a, ROW_TILE, A_KEY_TILE), lambda i: (i, 0, 0, 0)),
              jax.ShapeDtypeStruct((nt, n_a, ROW_TILE, A_KEY_TILE), BF16))
    outs = [rows(512, BF16), rows(512, F32), rows(256, BF16), tile_a, rows(512, BF16), rows(256, BF16), tile_t,
            rows(256, F32), rows(512, BF16), rows(LANE, F32), rows(LANE, BF16), rows(LANE, F32),
            (pl.BlockSpec((H_I, ROW_TILE), lambda i: (0, i)), jax.ShapeDtypeStruct((H_I, n), F32)),
            rows(D_C, F32), rows(3 * D_MODEL, F32)]
    res = pl.pallas_call(
        _in_proj_kernel,
        grid=(nt,),
        in_specs=[row(D_MODEL), _const_spec((1, D_MODEL)), _const_spec((D_MODEL, N_PROJ)),
                  _const_spec((N_PROJ_T, D_MODEL)),
                  tab, tab, tab, pl.BlockSpec((ROW_TILE, 1), lambda i: (i % table_tiles, 0))],
        out_specs=[o[0] for o in outs],
        out_shape=[o[1] for o in outs],
        compiler_params=_params(("parallel",)),
        name="in_proj",
    )(x, g, w, wt, cos, sa, sb, valid)
    return dict(zip(IN_PROJ_OUTS, res))


def _prep_w_in(w):
    o_aq, o_ak, o_av, o_bq, o_bk, o_bv, o_iq, o_iw, o_ik, o_cin, o_gz = (
        0, 512, 768, 1024, 1536, 1664, 1792, 2304, 2312, 2376, 3400)
    bk0, bk1 = w[:, o_bk:o_bk + 64], w[:, o_bk + 64:o_bk + 128]
    bv0, bv1 = w[:, o_bv:o_bv + 64], w[:, o_bv + 64:o_bv + 128]
    ik = w[:, o_ik:o_ik + D_I]
    iw = w[:, o_iw:o_iw + H_I]
    cols = [w[:, o_aq:o_aq + 512], w[:, o_ak:o_ak + 256], w[:, o_av:o_av + 256], w[:, o_bq:o_bq + 512],
            bk0, bk0, bk1, bk1,
            w[:, o_bk:o_bk + 128], w[:, o_bv:o_bv + 128],
            w[:, o_iq:o_iq + 512], ik, ik,
            iw, jnp.zeros((w.shape[0], LANE - H_I), w.dtype),
            w[:, o_cin:o_cin + 2 * D_C], w[:, o_gz:o_gz + 3 * D_MODEL]]
    out = jnp.concatenate(cols, axis=1).astype(BF16)
    cols_t = [w[:, o_av:o_av + 256], bv0, bv0, bv1, bv1, iw,
              jnp.zeros((w.shape[0], N_PROJ_T - R_IWT - H_I), w.dtype)]
    out_t = jnp.concatenate(cols_t, axis=1).T.astype(BF16)
    assert out.shape[1] == N_PROJ and out_t.shape[0] == N_PROJ_T
    return out, out_t


def _rotary_tables(pos):
    rot = 2 * ROT_HALF
    inv = jnp.power(ROPE_THETA, -jnp.arange(ROT_HALF, dtype=F32) * 2.0 / rot)
    ang = pos.astype(F32)[:, None] * inv[None, :]
    cos, sin = jnp.cos(ang), jnp.sin(ang)
    t = pos.shape[0]
    rest = 64 - rot
    one = jnp.ones((t, rest), F32)
    zero = jnp.zeros((t, rest), F32)
    zh = jnp.zeros((t, ROT_HALF), F32)
    c64 = jnp.concatenate([cos, cos, one], axis=1)
    sa64 = jnp.concatenate([-sin, zh, zero], axis=1)
    sb64 = jnp.concatenate([zh, sin, zero], axis=1)
    dup = lambda a: jnp.concatenate([a, a], axis=1)
    return dup(c64), dup(sa64), dup(sb64)


def _split_pair(pair):
    lo = _lane_ids(pair.shape) < 64
    zero = jnp.zeros_like(pair)
    return jnp.where(lo, pair, zero), jnp.where(lo, zero, pair)


def _monotone_key(score):
    b = lax.bitcast_convert_type(score, I32)
    return jnp.where(b < 0, b ^ I32(0x7FFFFFFF), b)


Q_PAIR = 2 * Q_TILE


def _pair_up(q_groups):
    return [jnp.concatenate(q_groups[n:n + 2], axis=0) for n in range(0, len(q_groups), 2)]


def _two_tile_pipeline(prefetch, consume, state, j_first, j_last, mask_middle):
    n_trips = (j_last - j_first + 2) // 2
    carry = (prefetch(0, j_first, True), state)
    segments = ((0, 1, True), (1, n_trips - 2, mask_middle), (jnp.maximum(n_trips - 2, 1), n_trips, True))
    for first, stop, masked in segments:
        def body(jj, c, masked=masked):
            t0 = j_first + 2 * jj
            aux_b = prefetch(1, t0 + 1, masked)
            st = consume(0, t0, c[0], c[1], masked)
            aux_a = prefetch(0, t0 + 2, masked)
            return aux_a, consume(1, t0 + 1, aux_b, st, masked)
        carry = lax.fori_loop(first, stop, body, carry)
    return carry[1]


def _flash_run_t(load_k, load_vt, bias_of, q_pairs, j_first, j_last, mask_middle, s_ref):
    kv_of = [kv for kv, pairs in enumerate(q_pairs) for _ in pairs]
    qs = [q for pairs in q_pairs for q in pairs]

    def prefetch(slot, j, masked):
        jc = jnp.minimum(j, j_last)
        if masked:
            bias = bias_of(j)
            bias = jnp.concatenate([bias, bias], axis=1)
        k = [load_k(jc, kv) for kv in range(len(q_pairs))]
        cmax = []
        for n, (kv, q) in enumerate(zip(kv_of, qs)):
            s = _nt_dot(k[kv], q)
            if masked:
                s = s + bias
            s_ref[slot, n] = s
            cmax.append(jnp.max(s, axis=0, keepdims=True))
        return cmax

    def step(slot, j, cmax, state, masked):
        del masked
        jc = jnp.minimum(j, j_last)
        v_t = [load_vt(jc, kv) for kv in range(len(q_pairs))]
        out = []
        for n, (kv, (m, l, acc)) in enumerate(zip(kv_of, state)):
            m_new = jnp.maximum(m, cmax[n])
            alpha = jnp.exp2(m - m_new)
            p = jnp.exp2(s_ref[slot, n] - m_new)
            l = alpha * l + jnp.sum(p, axis=0, keepdims=True)
            acc = alpha * acc + jnp.dot(v_t[kv], p.astype(BF16), preferred_element_type=F32)
            out.append((m_new, l, acc))
        return out

    state = [(jnp.full((1, Q_PAIR), M_FLOOR, F32), jnp.zeros((1, Q_PAIR), F32), jnp.zeros((LANE, Q_PAIR), F32))
             for _ in kv_of]
    state = _two_tile_pipeline(prefetch, step, state, j_first, j_last, mask_middle)
    return [acc / jnp.where(l > 0.0, l, 1.0) for _, l, acc in state]


def _attn_a_kernel(lam_ref, q_ref, k_ref, vt_ref, sub_ref, o_ref, s_ref, *, pad, out_scale):
    i = pl.program_id(1)
    q0 = i * Q_TILE

    @pl.when(q0 + Q_TILE <= pad)
    def _():
        o_ref[...] = jnp.zeros_like(o_ref)

    @pl.when(q0 + Q_TILE > pad)
    def _():
        lam = lam_ref[0]
        kt = A_KEY_TILE
        j_pad = pad // kt
        j_diag = (q0 + Q_TILE - 1) // kt
        qpos = _lane_ids((kt, Q_TILE)) + q0
        krow = _row_ids((kt, Q_TILE))
        q_pairs = []
        for kv in range(KV_A):
            q_groups = []
            for g in range(G_A):
                c0 = (kv * G_A + g) * LANE
                q_groups += list(_split_pair(q_ref[:, c0:c0 + LANE]))
            q_pairs.append(_pair_up(q_groups))

        def load_k(j, kv):
            return k_ref[pl.ds(pl.multiple_of(j * kt, kt), kt), kv * LANE:(kv + 1) * LANE]

        def load_vt(j, kv):
            return vt_ref[j, kv * LANE:(kv + 1) * LANE, :]

        def bias_of(j):
            kpos = krow + j * kt
            return jnp.where((kpos <= qpos) & (kpos >= pad), 0.0, MASKED)

        outs = _flash_run_t(load_k, load_vt, bias_of, q_pairs, j_pad, j_diag, False, s_ref)
        for kv in range(KV_A):
            o = outs[kv * G_A:(kv + 1) * G_A]
            for g in range(G_A):
                y = o[g][:, :Q_TILE] - lam * o[g][:, Q_TILE:]
                y = y * lax.rsqrt(jnp.mean(y * y, axis=0, keepdims=True) + EPS) * sub_ref[...] * out_scale
                c0 = (kv * G_A + g) * LANE
                o_ref[:, c0:c0 + LANE] = y.T.astype(o_ref.dtype)


def _attn_a_prompt(lam, aq, kab, avt, subln_col, pad, out_scale):
    b, tp, _ = aq.shape
    return pl.pallas_call(
        functools.partial(_attn_a_kernel, pad=pad, out_scale=out_scale),
        grid=(b, tp // Q_TILE),
        in_specs=[pl.BlockSpec(memory_space=pltpu.SMEM),
                  pl.BlockSpec((None, Q_TILE, 512), lambda bi, i: (bi, i, 0)),
                  pl.BlockSpec((None, tp, 256), lambda bi, i: (bi, 0, 0)),
                  pl.BlockSpec((None, tp // A_KEY_TILE, 256, A_KEY_TILE), lambda bi, i: (bi, 0, 0, 0)),
                  _const_spec((LANE, 1))],
        out_specs=pl.BlockSpec((None, Q_TILE, 512), lambda bi, i: (bi, i, 0)),
        out_shape=jax.ShapeDtypeStruct((b, tp, 512), BF16),
        scratch_shapes=[pltpu.VMEM((2, KV_A * G_A, A_KEY_TILE, Q_PAIR), F32)],
        compiler_params=_params(("parallel", "parallel")),
        name="attn_a_prompt",
    )(lam, aq, kab, avt, subln_col)


def _kth_largest_key(count_ge, k_sel, n_vis, score_min, score_max, passes_per_check):
    kf = float(k_sel)
    key_min, key_max = _monotone_key(score_min), _monotone_key(score_max)
    no_key = _monotone_key(jnp.full(n_vis.shape, NO_SCORE, F32))
    zero = jnp.zeros_like(key_min)
    one = zero + 1
    c_pos = count_ge(one)
    c_nn = count_ge(zero)
    few = n_vis <= kf
    pos = c_pos >= kf
    neg = c_nn < kf
    lo = jnp.where(pos, one, jnp.where(neg, key_min, zero))
    hi = jnp.where(pos, key_max + 1, jnp.where(neg, zero, one))
    c_lo = jnp.where(pos, c_pos, jnp.where(neg, n_vis, c_nn))
    c_hi = jnp.where(pos, 0.0, jnp.where(neg, c_nn, c_pos))
    lo = jnp.where(few, no_key + 1, lo)
    done = few | (c_lo == kf) | (hi - 1 <= lo)
    ones = jnp.ones_like(n_vis)

    def cond(st):
        it, active = st[0], st[-1]
        return (it < SEARCH_CAP) & (jnp.sum(active) > 0.0)

    def body(st):
        it, lo, hi, c_lo, c_hi, f_lo, f_hi, last, active = st
        done = active == 0.0
        g_lo = (c_lo - (kf - 0.5)) * f_lo
        g_hi = ((kf - 0.5) - c_hi) * f_hi
        frac = jnp.where(done, 0.5, g_hi / (g_lo + g_hi))
        lo_f, hi_f = lo.astype(F32), hi.astype(F32)
        t_f = jnp.clip(hi_f - (hi_f - lo_f) * frac, -KEY_F_MAX, KEY_F_MAX)
        t_mid = (lo >> 1) + (hi >> 1) + (lo & hi & 1)
        halve = (f_lo < 0.2) | (f_hi < 0.2) | (c_lo - c_hi <= 4.0)
        t = jnp.where(halve, t_mid, t_f.astype(I32))
        t = jnp.minimum(jnp.maximum(t, lo + 1), hi - 1)
        c = count_ge(t)
        up = c >= kf
        act_up = jnp.logical_and(jnp.logical_not(done), up)
        act_dn = jnp.logical_and(jnp.logical_not(done), jnp.logical_not(up))
        f_lo = jnp.where(act_dn & (last == -1), f_lo * 0.5, jnp.where(act_up, ones, f_lo))
        f_hi = jnp.where(act_up & (last == 1), f_hi * 0.5, jnp.where(act_dn, ones, f_hi))
        last = jnp.where(act_up, 1, jnp.where(act_dn, -1, last))
        lo = jnp.where(act_up, t, lo)
        c_lo = jnp.where(act_up, c, c_lo)
        hi = jnp.where(act_dn, t, hi)
        c_hi = jnp.where(act_dn, c, c_hi)
        done = done | (c_lo == kf) | (hi - 1 <= lo)
        return it + 1, lo, hi, c_lo, c_hi, f_lo, f_hi, last, jnp.where(done, 0.0, 1.0)

    def trip(s):
        for _ in range(passes_per_check):
            s = body(s)
        return s

    st = lax.while_loop(cond, trip,
                        (I32(0), lo, hi, c_lo, c_hi, ones, ones, zero, jnp.where(done, 0.0, 1.0)))
    return st[1], st[3]


WIDE = 8


def _kth_largest_key_wide(keys, k_sel, n_vis, score_min, score_max):
    r, n = keys.shape
    kf = float(k_sel)
    rep = lambda a: jnp.concatenate([jnp.broadcast_to(a[q:q + 1], (WIDE, a.shape[1])) for q in range(r)], axis=0)
    group_sum = lambda a: rep(jnp.sum(a.reshape(r, WIDE, 1), axis=1))
    keys_w = rep(keys)
    cand = (_row_ids((r * WIDE, 1)) % WIDE).astype(F32)
    frac = (cand + 1.0) * (1.0 / (WIDE + 1))
    no_key = _monotone_key(jnp.full((r, 1), NO_SCORE, F32))
    few = n_vis <= kf
    lo = rep(jnp.where(few, no_key + 1, _monotone_key(score_min)))
    hi = rep(_monotone_key(score_max) + 1)
    c_lo = rep(n_vis)
    c_hi = jnp.zeros_like(c_lo)
    done = (c_lo <= kf) | (hi - 1 <= lo)

    def cond(st):
        return (st[0] < SEARCH_CAP) & (jnp.sum(st[-1]) > 0.0)

    def pick(sel, t, cnt):
        take = lambda a: group_sum(jnp.where(sel, a, 0.0))
        t_sel = (take((t >> 16).astype(F32)).astype(I32) << 16) | take((t & 0xFFFF).astype(F32)).astype(I32)
        return t_sel, take(cnt)

    def body(st):
        it, lo, hi, c_lo, c_hi, active = st
        lo_f, hi_f = lo.astype(F32), hi.astype(F32)
        narrow = hi_f - lo_f < 1.6e7
        w_f = jnp.where(narrow, (hi - lo).astype(F32), hi_f - lo_f)
        t = jnp.where(narrow, lo + (w_f * frac).astype(I32),
                      jnp.clip(lo_f + w_f * frac, -KEY_F_MAX, KEY_F_MAX).astype(I32))
        t = jnp.minimum(jnp.maximum(t, lo + 1), hi - 1)
        cnt = jnp.sum(jnp.where(keys_w >= t, 1.0, 0.0), axis=1, keepdims=True)
        live = active > 0.0
        n_up = group_sum(jnp.where(cnt >= kf, 1.0, 0.0))
        t_up, c_up = pick(cand == n_up - 1.0, t, cnt)
        t_dn, c_dn = pick(cand == n_up, t, cnt)
        raise_lo = live & (n_up >= 1.0)
        lower_hi = live & (n_up < float(WIDE))
        lo = jnp.where(raise_lo, t_up, lo)
        c_lo = jnp.where(raise_lo, c_up, c_lo)
        hi = jnp.where(lower_hi, t_dn, hi)
        c_hi = jnp.where(lower_hi, c_dn, c_hi)
        done = jnp.logical_not(live) | (c_lo == kf) | (hi - 1 <= lo)
        return it + 1, lo, hi, c_lo, c_hi, jnp.where(done, 0.0, 1.0)

    st = lax.while_loop(cond, lambda s: body(body(s)), (I32(0), lo, hi, c_lo, c_hi, jnp.where(done, 0.0, 1.0)))
    first = lambda a: jnp.sum(jnp.where(cand == 0.0, a, 0.0).reshape(r, WIDE, 1), axis=1)
    lo = st[1]
    thr = (first((lo >> 16).astype(F32)).astype(I32) << 16) | first((lo & 0xFFFF).astype(F32)).astype(I32)
    return thr, first(st[3])


def _tie_cut(count_tied_before, n_keep, n_pos_bits):
    def step(it, x):
        cand = x + lax.shift_left(I32(1), I32(n_pos_bits - 1) - it)
        return jnp.where(count_tied_before(cand) < n_keep, cand, x)

    return lax.fori_loop(0, n_pos_bits, step, jnp.zeros(n_keep.shape, I32))


def _attn_b_kernel(iq_ref, iwt_ref, q_ref, ik_ref, kk_ref, vvt_ref, o_ref, key_ref, s_ref, d_ref, *, pad, k_sel):
    i = pl.program_id(1)
    q0 = i * Q_TILE

    @pl.when(q0 + Q_TILE <= pad)
    def _():
        o_ref[...] = jnp.zeros_like(o_ref)

    @pl.when(q0 + Q_TILE > pad)
    def _():
        j_diag = (q0 + Q_TILE - 1) // KEY_TILE
        n_kb = j_diag + 1
        qpos = _lane_ids((KEY_TILE, Q_TILE)) + q0
        krow = _row_ids((KEY_TILE, Q_TILE))
        no_key = _monotone_key(jnp.full((1, 1), NO_SCORE, F32))
        def fold(a, op):
            parts = [a[r * SUBLANE:(r + 1) * SUBLANE] for r in range(KEY_TILE // SUBLANE)]
            while len(parts) > 1:
                parts = [op(parts[n], parts[n + 1]) for n in range(0, len(parts), 2)]
            return parts[0]

        parts = []
        for pr in range(H_I // 2):
            parts += list(_split_pair(iq_ref[:, pr * LANE:(pr + 1) * LANE]))
        qi = jnp.concatenate(parts, axis=0)
        iwt = iwt_ref[...]

        def dots(slot, j, masked):
            k0 = pl.multiple_of(jnp.minimum(j, j_diag) * KEY_TILE, KEY_TILE)
            d_ref[slot] = _nt_dot(ik_ref[pl.ds(k0, KEY_TILE), :], qi)
            return ()

        def score_tile(slot, j, _, carry, masked):
            kmin, kmax = carry
            sc = jnp.zeros((KEY_TILE, Q_TILE), F32)
            for h in range(H_I):
                sc = sc + iwt[h:h + 1, :] * jnp.maximum(d_ref[slot, :, h * Q_TILE:(h + 1) * Q_TILE], 0.0)
            sc = sc + 0.0
            sc_lo = sc_hi = sc
            key = _monotone_key(sc)
            if masked:
                kpos = krow + j * KEY_TILE
                vis = (kpos <= qpos) & (kpos >= pad)
                key = jnp.where(vis, key, no_key)
                sc_lo = jnp.where(vis, sc, -NO_SCORE)
                sc_hi = jnp.where(vis, sc, NO_SCORE)
            key_ref[j] = key
            return jnp.minimum(kmin, fold(sc_lo, jnp.minimum)), jnp.maximum(kmax, fold(sc_hi, jnp.maximum))

        carry = (jnp.full((SUBLANE, Q_TILE), -NO_SCORE, F32), jnp.full((SUBLANE, Q_TILE), NO_SCORE, F32))
        carry = _two_tile_pipeline(dots, score_tile, carry, 0, j_diag, False)
        sc_min = jnp.min(carry[0], axis=0, keepdims=True)
        sc_max = jnp.max(carry[1], axis=0, keepdims=True)

        key_ref[n_kb] = jnp.broadcast_to(no_key, (KEY_TILE, Q_TILE))

        def count_ge(t):
            def cbody(jj, c):
                for h in range(2):
                    c = c + fold(jnp.where(key_ref[2 * jj + h] >= t, 1.0, 0.0), jnp.add)
                return c
            c = lax.fori_loop(0, (n_kb + 1) // 2, cbody, jnp.zeros((SUBLANE, Q_TILE), F32))
            return jnp.sum(c, axis=0, keepdims=True)

        n_vis = (_lane_ids((1, Q_TILE)) + (q0 - pad + 1)).astype(F32)
        thr, c_thr = _kth_largest_key(count_ge, k_sel, n_vis, sc_min, sc_max, 2)
        tied = c_thr > float(k_sel)

        @pl.when(jnp.sum(jnp.where(tied, 1.0, 0.0)) > 0.0)
        def _():
            n_keep = float(k_sel) - count_ge(thr + 1)

            def tied_before(x):
                def cbody(j, c):
                    hit = (key_ref[j] == thr) & (krow + j * KEY_TILE < x)
                    return c + fold(jnp.where(hit, 1.0, 0.0), jnp.add)
                c = lax.fori_loop(0, n_kb, cbody, jnp.zeros((SUBLANE, Q_TILE), F32))
                return jnp.sum(c, axis=0, keepdims=True)

            x = _tie_cut(tied_before, n_keep, (key_ref.shape[0] * KEY_TILE).bit_length())

            def lower(j, _):
                key = key_ref[j]
                drop = tied & (key == thr) & (krow + j * KEY_TILE > x)
                key_ref[j] = jnp.where(drop, key - 1, key)
                return 0

            lax.fori_loop(0, n_kb, lower, 0)

        q_pairs = []
        for kv in range(KV_B):
            q_groups = []
            for pr in range(G_B // 2):
                c0 = (kv * (G_B // 2) + pr) * LANE
                q_groups += list(_split_pair(q_ref[:, c0:c0 + LANE]))
            q_pairs.append(_pair_up(q_groups))

        def load_k(j, kv):
            return kk_ref[pl.ds(pl.multiple_of(j * KEY_TILE, KEY_TILE), KEY_TILE), kv * LANE:(kv + 1) * LANE]

        def load_vt(j, kv):
            return vvt_ref[j, kv * LANE:(kv + 1) * LANE, :]

        def bias_of(j):
            thr_j = jnp.where(j <= j_diag, thr, INT_MAX)
            return jnp.where(key_ref[jnp.minimum(j, j_diag)] >= thr_j, 0.0, MASKED)

        outs = _flash_run_t(load_k, load_vt, bias_of, q_pairs, 0, j_diag, True, s_ref)
        lo = _lane_ids((Q_TILE, LANE)) < 64
        for kv in range(KV_B):
            o = outs[kv * (G_B // 2):(kv + 1) * (G_B // 2)]
            for pr in range(G_B // 2):
                pair = jnp.where(lo, o[pr][:, :Q_TILE].T, o[pr][:, Q_TILE:].T)
                c0 = (kv * (G_B // 2) + pr) * LANE
                o_ref[:, c0:c0 + LANE] = pair.astype(o_ref.dtype)


def _attn_b_prompt(iq, iwt, bq, ikkb, bkk, bvt, pad, k_sel):
    b, tp, _ = bq.shape
    nq = tp // Q_TILE
    qspec = lambda width: pl.BlockSpec((None, Q_TILE, width), lambda bi, i: (bi, i, 0))
    full = lambda width: pl.BlockSpec((None, tp, width), lambda bi, i: (bi, 0, 0))
    return pl.pallas_call(
        functools.partial(_attn_b_kernel, pad=pad, k_sel=k_sel),
        grid=(b, nq),
        in_specs=[qspec(512), pl.BlockSpec((H_I, Q_TILE), lambda bi, i: (0, bi * nq + i)), qspec(512),
                  full(LANE), full(256),
                  pl.BlockSpec((None, tp // KEY_TILE, 256, KEY_TILE), lambda bi, i: (bi, 0, 0, 0))],
        out_specs=qspec(512),
        out_shape=jax.ShapeDtypeStruct((b, tp, 512), BF16),
        scratch_shapes=[pltpu.VMEM((tp // KEY_TILE + 1, KEY_TILE, Q_TILE), I32),
                        pltpu.VMEM((2, KV_B * G_B // 2, KEY_TILE, Q_PAIR), F32),
                        pltpu.VMEM((2, KEY_TILE, H_I * Q_TILE), F32)],
        compiler_params=_params(("parallel", "parallel")),
        name="attn_b_prompt",
    )(iq, iwt, bq, ikkb, bkk, bvt)


def _ln_swish(y, g, b):
    mu = jnp.mean(y, axis=-1, keepdims=True)
    yc = y - mu
    z = yc * lax.rsqrt(jnp.mean(yc * yc, axis=-1, keepdims=True) + LN_EPS) * g + b
    return z * jax.nn.sigmoid(z)


HALO = 32


def _conv_c_prompt_kernel(prev_ref, cur_ref, w_ref, b_ref, g_ref, beta_ref, o_ref, ext_ref):
    ext_ref[0:HALO, :] = prev_ref[ROW_TILE - HALO:ROW_TILE, :]
    ext_ref[HALO:HALO + ROW_TILE, :] = cur_ref[...]
    acc = jnp.zeros((ROW_TILE, D_C), F32) + b_ref[...]
    base = HALO - (CONV_C - 1)
    for j in range(CONV_C):
        acc = acc + w_ref[j:j + 1, :] * ext_ref[base + j:base + j + ROW_TILE, :]
    o_ref[...] = _ln_swish(acc, g_ref[...], beta_ref[...]).astype(o_ref.dtype)


def _conv_c_prompt(u, w, b, g, beta):
    n = u.shape[0]
    return pl.pallas_call(
        _conv_c_prompt_kernel,
        grid=(n // ROW_TILE,),
        in_specs=[pl.BlockSpec((ROW_TILE, D_C), lambda i: (jnp.maximum(i - 1, 0), 0)),
                  pl.BlockSpec((ROW_TILE, D_C), lambda i: (i, 0)),
                  _const_spec((CONV_C, D_C)), _const_spec((1, D_C)), _const_spec((1, D_C)),
                  _const_spec((1, D_C))],
        out_specs=pl.BlockSpec((ROW_TILE, D_C), lambda i: (i, 0)),
        out_shape=jax.ShapeDtypeStruct((n, D_C), BF16),
        scratch_shapes=[pltpu.VMEM((HALO + ROW_TILE, D_C), F32)],
        compiler_params=_params(("parallel",)),
        name="conv_c_prompt",
    )(u, u, w, b, g, beta)


def _conv_c_sample_kernel(st_ref, u_ref, w_ref, b_ref, g_ref, beta_ref, o_ref):
    n_hist = st_ref.shape[0]
    for t in range(u_ref.shape[0]):
        acc = jnp.zeros(o_ref.shape[1:], F32) + b_ref[...]
        for j in range(CONV_C):
            e = t + j
            row = st_ref[e] if e < n_hist else u_ref[e - n_hist]
            acc = acc + w_ref[j:j + 1, :] * row
        o_ref[t] = _ln_swish(acc, g_ref[...], beta_ref[...]).astype(o_ref.dtype)


def _conv_c_sample(st_t, u_t, w, b, g, beta):
    n_new, n_seq, _ = u_t.shape
    return pl.pallas_call(
        _conv_c_sample_kernel,
        grid=(1,),
        in_specs=[_const_spec(st_t.shape), _const_spec(u_t.shape), _const_spec((CONV_C, D_C)),
                  _const_spec((1, D_C)), _const_spec((1, D_C)), _const_spec((1, D_C))],
        out_specs=_const_spec((n_new, n_seq, D_C)),
        out_shape=jax.ShapeDtypeStruct((n_new, n_seq, D_C), BF16),
        compiler_params=_params(("arbitrary",)),
        name="conv_c_sample",
    )(st_t, u_t, w, b, g, beta)


def _merge_kernel(ya_ref, yb_ref, yc_ref, gz_ref, x_ref, wa_ref, wb_ref, wc_ref, wo_ref, o_ref):
    merged = jnp.zeros((ROW_TILE, D_MODEL), F32)
    for k, (y_ref, w_ref) in enumerate(((ya_ref, wa_ref), (yb_ref, wb_ref), (yc_ref, wc_ref))):
        gate = jax.nn.sigmoid(gz_ref[:, k * D_MODEL:(k + 1) * D_MODEL])
        merged = merged + gate * jnp.dot(y_ref[...], w_ref[...], preferred_element_type=F32)
    o_ref[...] = x_ref[...] + jnp.dot(merged.astype(BF16), wo_ref[...], preferred_element_type=F32)


def _merge(ya, yb, yc, gz, x, wa, wb, wc, wo):
    n = x.shape[0]
    row = lambda width: pl.BlockSpec((ROW_TILE, width), lambda i: (i, 0))
    return pl.pallas_call(
        _merge_kernel,
        grid=(n // ROW_TILE,),
        in_specs=[row(512), row(512), row(D_C), row(3 * D_MODEL), row(D_MODEL),
                  _const_spec(wa.shape), _const_spec(wb.shape), _const_spec(wc.shape), _const_spec(wo.shape)],
        out_specs=row(D_MODEL),
        out_shape=jax.ShapeDtypeStruct((n, D_MODEL), F32),
        compiler_params=_params(("parallel",)),
        name="merge",
    )(ya, yb, yc, gz, x, wa, wb, wc, wo)


TAIL = 8


def _ffn_kernel(*refs, carried, final_norm, n_new):
    if carried:
        (x_ref, g_ref, wu_ref, cw_ref, cb_ref, wd_ref, gf_ref, valid_ref,
         o_ref, tail_ref, carry_ref) = refs
    else:
        (x_ref, g_ref, wu_ref, cw_ref, cb_ref, wd_ref, gf_ref, h1_ref, h2_ref,
         o_ref, up_ref) = refs
    x = x_ref[...]
    tm = x.shape[0]
    xb = _rms(x, g_ref[...]).astype(BF16)
    row = _row_ids((tm, 1))

    if carried:
        @pl.when(pl.program_id(0) == 0)
        def _():
            carry_ref[...] = jnp.zeros_like(carry_ref)
        valid = valid_ref[...]
    else:
        t_in_seq = row % n_new

    def conv(c0):
        up = jnp.dot(xb, wu_ref[:, c0:c0 + FF_CHUNK], preferred_element_type=F32)
        if carried:
            up = up * valid
            old = carry_ref[:, c0:c0 + FF_CHUNK]
            p1 = jnp.where(row == 0, old[TAIL - 1:TAIL], pltpu.roll(up, 1, 0))
            p2 = jnp.where(row == 0, old[TAIL - 2:TAIL - 1],
                           jnp.where(row == 1, old[TAIL - 1:TAIL], pltpu.roll(up, 2, 0)))
            carry_ref[:, c0:c0 + FF_CHUNK] = up[tm - TAIL:]
            for r in range(tm // ROW_TILE):
                tail_ref[r, :, c0:c0 + FF_CHUNK] = up[(r + 1) * ROW_TILE - TAIL:(r + 1) * ROW_TILE]
        else:
            p1 = jnp.where(t_in_seq < 1, h1_ref[:, c0:c0 + FF_CHUNK], pltpu.roll(up, 1, 0))
            p2 = jnp.where(t_in_seq < 2, h2_ref[:, c0:c0 + FF_CHUNK], pltpu.roll(up, 2, 0))
            up_ref[:, c0:c0 + FF_CHUNK] = up
        w = cw_ref[:, c0:c0 + FF_CHUNK]
        return w[0:1] * p2 + w[1:2] * p1 + w[2:3] * up + cb_ref[:, c0:c0 + FF_CHUNK]

    acc = jnp.zeros((tm, D_MODEL), F32)
    for c in range(D_FF // FF_CHUNK):
        hg = conv(c * FF_CHUNK)
        hv = conv(D_FF + c * FF_CHUNK)
        h = (hg * jax.nn.sigmoid(hg) * hv).astype(BF16)
        acc = acc + jnp.dot(h, wd_ref[c * FF_CHUNK:(c + 1) * FF_CHUNK, :], preferred_element_type=F32)
    y = x + acc
    o_ref[...] = _rms(y, gf_ref[...]) if final_norm else y


def _ffn(x, g, wu, cw, cb, wd, gf, extra, carried, final_norm, n_new=1):
    n = x.shape[0]
    tm = FFN_TILE if carried and n % FFN_TILE == 0 else ROW_TILE
    nt = n // tm
    row = lambda width: pl.BlockSpec((tm, width), lambda i: (i, 0))
    in_specs = [row(D_MODEL), _const_spec((1, D_MODEL)), _const_spec(wu.shape), _const_spec(cw.shape),
                _const_spec(cb.shape), _const_spec(wd.shape), _const_spec((1, D_MODEL))]
    if carried:
        halves = tm // ROW_TILE
        in_specs += [pl.BlockSpec((tm, 1), lambda i: (i, 0))]
        out_specs = [row(D_MODEL), pl.BlockSpec((None, halves, TAIL, 2 * D_FF), lambda i: (i, 0, 0, 0))]
        out_shape = [jax.ShapeDtypeStruct((n, D_MODEL), F32),
                     jax.ShapeDtypeStruct((nt, halves, TAIL, 2 * D_FF), F32)]
        scratch = [pltpu.VMEM((TAIL, 2 * D_FF), F32)]
    else:
        in_specs += [row(2 * D_FF), row(2 * D_FF)]
        out_specs = [row(D_MODEL), row(2 * D_FF)]
        out_shape = [jax.ShapeDtypeStruct((n, D_MODEL), F32), jax.ShapeDtypeStruct((n, 2 * D_FF), F32)]
        scratch = []
    return pl.pallas_call(
        functools.partial(_ffn_kernel, carried=carried, final_norm=final_norm, n_new=n_new),
        grid=(nt,),
        in_specs=in_specs, out_specs=out_specs, out_shape=out_shape, scratch_shapes=scratch,
        compiler_params=_params(("arbitrary",)),
        name="ffn_prompt" if carried else "ffn_sample",
    )(x, g, wu, cw, cb, wd, gf, *extra)


S_ROWS = 8


def _new_key_mask(shape, n_past, n_new):
    col = _lane_ids(shape)
    t = _row_ids(shape) % S_ROWS
    return (col < n_past) | ((col - n_past <= t) & (col < n_past + n_new))


def _pad_rows(a, rows):
    return jnp.concatenate([a, jnp.zeros((rows - a.shape[0], a.shape[1]), a.dtype)], axis=0)


def _page_specs(block, layer, n_pages):
    nd = len(block)
    return [pl.BlockSpec((None, None) + block,
                         lambda s, pt, j=j: (layer, pt[s * n_pages + j]) + (0,) * nd) for j in range(n_pages)]


def _attn_a_sample_kernel(pt_ref, lam_ref, q_ref, new_ref, sub_ref, *rest, n_pages, page, n_new, out_scale):
    page_refs, o_ref = rest[:n_pages], rest[n_pages]
    lam = lam_ref[0]
    n_past = n_pages * page
    q = q_ref[...].astype(F32)
    new = _pad_rows(new_ref[...], LANE)
    for kv in range(KV_A):
        parts = []
        for g in range(G_A):
            c0 = (kv * G_A + g) * LANE
            parts += list(_split_pair(q[:, c0:c0 + LANE]))
        qs = jnp.concatenate(parts, axis=0).astype(BF16)
        ksl = slice(kv * LANE, (kv + 1) * LANE)
        vsl = slice((KV_A + kv) * LANE, (KV_A + kv + 1) * LANE)
        k = jnp.concatenate([r[pl.ds(kv, page, stride=2 * KV_A), :].astype(BF16) for r in page_refs]
                            + [new[:, ksl].astype(BF16)], axis=0)
        v = jnp.concatenate([r[pl.ds(KV_A + kv, page, stride=2 * KV_A), :].astype(BF16) for r in page_refs]
                            + [new[:, vsl].astype(BF16)], axis=0)
        s = _nt_dot(qs, k)
        s = jnp.where(_new_key_mask(s.shape, n_past, n_new), s, MASKED)
        p = jnp.exp2(s - jnp.max(s, axis=1, keepdims=True))
        o = jnp.dot(p.astype(BF16), v, preferred_element_type=F32) / jnp.sum(p, axis=1, keepdims=True)
        for g in range(G_A):
            b0 = g * 2 * S_ROWS
            y = o[b0:b0 + S_ROWS] - lam * o[b0 + S_ROWS:b0 + 2 * S_ROWS]
            y = _rms(y, sub_ref[...]) * out_scale
            c0 = (kv * G_A + g) * LANE
            o_ref[:, c0:c0 + LANE] = y.astype(o_ref.dtype)


def _attn_a_sample(pt, lam, aq, new, subln, cache, layer, n_new, out_scale):
    n_seq = aq.shape[0]
    n_pages = pt.shape[0] // n_seq
    page = cache.shape[2] // (2 * KV_A)
    seq = lambda width: pl.BlockSpec((None, S_ROWS, width), lambda s, pt: (s, 0, 0))
    grid_spec = pltpu.PrefetchScalarGridSpec(
        num_scalar_prefetch=1,
        grid=(n_seq,),
        in_specs=[pl.BlockSpec(memory_space=pltpu.SMEM), seq(512), seq(512),
                  pl.BlockSpec((1, LANE), lambda s, pt: (0, 0))]
        + _page_specs((page * 2 * KV_A, LANE), layer, n_pages),
        out_specs=seq(512),
    )
    return pl.pallas_call(
        functools.partial(_attn_a_sample_kernel, n_pages=n_pages, page=page, n_new=n_new, out_scale=out_scale),
        grid_spec=grid_spec,
        out_shape=jax.ShapeDtypeStruct((n_seq, S_ROWS, 512), BF16),
        compiler_params=_params(("parallel",)),
        name="attn_a_sample",
    )(pt, lam, aq, new, subln, *([cache] * n_pages))


def _attn_b_sample_kernel(pt_ref, iq_ref, iw_ref, q_ref, newkv_ref, newik_ref, *rest,
                          n_pages, page, n_new, k_sel):
    kv_refs, ik_refs, o_ref = rest[:n_pages], rest[n_pages:2 * n_pages], rest[2 * n_pages]
    n_past = n_pages * page
    lo = _lane_ids((S_ROWS, LANE)) < 64

    def heads64(x):
        out = []
        for pr in range(x.shape[1] // LANE):
            pair = x[:, pr * LANE:(pr + 1) * LANE]
            out += [pair[:, :64], pltpu.roll(pair, 64, 1)[:, :64]]
        return out

    qi = jnp.concatenate(heads64(iq_ref[...].astype(F32)), axis=0).astype(BF16)
    ik_t = jnp.concatenate([r[...].astype(BF16) for r in ik_refs], axis=1)
    newik = _pad_rows(newik_ref[:, :D_I], LANE).astype(BF16)
    d = jnp.concatenate([jnp.dot(qi, ik_t, preferred_element_type=F32), _nt_dot(qi, newik)], axis=1)
    d = jnp.maximum(d, 0.0)
    iw = iw_ref[...]
    sc = jnp.zeros((S_ROWS, d.shape[1]), F32)
    for h in range(H_I):
        sc = sc + iw[:, h:h + 1] * d[h * S_ROWS:(h + 1) * S_ROWS]
    sc = sc + 0.0
    vis = _new_key_mask(sc.shape, n_past, n_new)
    no_key = _monotone_key(jnp.full((1, 1), NO_SCORE, F32))
    keys = jnp.where(vis, _monotone_key(sc), no_key)

    def count_ge(t):
        return jnp.sum(jnp.where(keys >= t, 1.0, 0.0), axis=1, keepdims=True)

    n_vis = jnp.sum(jnp.where(vis, 1.0, 0.0), axis=1, keepdims=True)
    sc_min = jnp.min(jnp.where(vis, sc, -NO_SCORE), axis=1, keepdims=True)
    sc_max = jnp.max(jnp.where(vis, sc, NO_SCORE), axis=1, keepdims=True)
    thr, c_thr = _kth_largest_key_wide(keys, k_sel, n_vis, sc_min, sc_max)
    tied = c_thr > float(k_sel)

    def break_ties(keys):
        n_keep = float(k_sel) - count_ge(thr + 1)
        kpos = _lane_ids(keys.shape)

        def tied_before(x):
            return jnp.sum(jnp.where((keys == thr) & (kpos < x), 1.0, 0.0), axis=1, keepdims=True)

        x = _tie_cut(tied_before, n_keep, keys.shape[1].bit_length())
        return jnp.where(tied & (keys == thr) & (kpos > x), keys - 1, keys)

    keys = lax.cond(jnp.sum(jnp.where(tied, 1.0, 0.0)) > 0.0, break_ties, lambda a: a, keys)
    bias = jnp.concatenate([jnp.where(keys >= thr, 0.0, MASKED)] * G_B, axis=0)

    q_heads = heads64(q_ref[...].astype(F32))
    newkv = _pad_rows(newkv_ref[...], LANE)
    newk128, newv128 = newkv[:, 0:LANE], newkv[:, LANE:2 * LANE]
    for kv in range(KV_B):
        qs = jnp.concatenate(q_heads[kv * G_B:(kv + 1) * G_B], axis=0).astype(BF16)
        k_t = jnp.concatenate([r[kv * 64:(kv + 1) * 64, :].astype(BF16) for r in kv_refs], axis=1)
        v_t = jnp.concatenate([r[(KV_B + kv) * 64:(KV_B + kv + 1) * 64, :].astype(BF16) for r in kv_refs],
                              axis=1)
        v_t2 = jnp.concatenate([v_t, v_t], axis=0)
        newk = (newk128 if kv == 0 else pltpu.roll(newk128, 64, 1))[:, :64].astype(BF16)
        s = jnp.concatenate([jnp.dot(qs, k_t, preferred_element_type=F32), _nt_dot(qs, newk)], axis=1) + bias
        m = jnp.maximum(jnp.max(s, axis=1, keepdims=True), M_FLOOR)
        p = jnp.exp2(s - m)
        l = jnp.sum(p, axis=1, keepdims=True)
        pb = p.astype(BF16)
        o_new = jnp.dot(pb[:, n_past:], newv128.astype(BF16), preferred_element_type=F32)
        swapped = pltpu.roll(o_new, 64, 1)
        lo4 = jnp.concatenate([lo] * G_B, axis=0)
        o_new = jnp.where(lo4, o_new, swapped) if kv == 0 else jnp.where(lo4, swapped, o_new)
        o = (_nt_dot(pb[:, :n_past], v_t2) + o_new) / jnp.where(l > 0.0, l, 1.0)
        for pr in range(G_B // 2):
            even = o[2 * pr * S_ROWS:(2 * pr + 1) * S_ROWS]
            odd = o[(2 * pr + 1) * S_ROWS:(2 * pr + 2) * S_ROWS]
            c0 = (kv * (G_B // 2) + pr) * LANE
            o_ref[:, c0:c0 + LANE] = jnp.where(lo, even, odd).astype(o_ref.dtype)


def _attn_b_sample(pt, iq, iw, bq, newkv, newik, cache_kv_t, cache_ik_t, layer, n_new, k_sel):
    n_seq = bq.shape[0]
    n_pages = pt.shape[0] // n_seq
    page = cache_kv_t.shape[3]
    seq = lambda width: pl.BlockSpec((None, S_ROWS, width), lambda s, pt: (s, 0, 0))
    grid_spec = pltpu.PrefetchScalarGridSpec(
        num_scalar_prefetch=1,
        grid=(n_seq,),
        in_specs=[seq(512), seq(LANE), seq(512), seq(256), seq(LANE)]
        + _page_specs((2 * KV_B * HD_B, page), layer, n_pages) + _page_specs((D_I, page), layer, n_pages),
        out_specs=seq(512),
    )
    return pl.pallas_call(
        functools.partial(_attn_b_sample_kernel, n_pages=n_pages, page=page, n_new=n_new, k_sel=k_sel),
        grid_spec=grid_spec,
        out_shape=jax.ShapeDtypeStruct((n_seq, S_ROWS, 512), BF16),
        compiler_params=_params(("parallel",)),
        name="attn_b_sample",
    )(pt, iq, iw, bq, newkv, newik, *([cache_kv_t] * n_pages), *([cache_ik_t] * n_pages))


def kernel(x_prompt, x_sample, cache_kv_a, cache_kv_b, cache_idx_b, state_conv_c, state_conv_ffn, page_table, meta_tokens, norm_mix, w_in, lam_q1, lam_k1, lam_q2, lam_k2, subln_a, w_out_a, w_out_b, conv_c_w, conv_c_b, ln_c_g, ln_c_b, w_out_c, w_o, norm_ffn, w_up, conv_f_w, conv_f_b, w_down, norm_final):
    depth = w_in.shape[0]
    n_prompt, seq, _ = x_prompt.shape
    t_len = seq + N_META
    pad = (-t_len) % ROW_TILE
    tp = t_len + pad
    n_seq, n_new, _ = x_sample.shape
    n_pool, page = cache_kv_a.shape[1:3]
    n_pages = page_table.shape[1]
    past_len = n_pages * page
    k_sel_p = min(TOPK_MAX, seq // 4)
    k_sel_s = min(TOPK_MAX, (past_len + n_new) // 4)
    assert n_new <= S_ROWS and ROW_TILE % n_new == 0 and (n_seq * n_new) % ROW_TILE == 0
    assert tp % ROW_TILE == 0 and KEY_TILE == ROW_TILE and page == LANE

    xp = jnp.concatenate([jnp.zeros((n_prompt, pad, D_MODEL), F32),
                          jnp.broadcast_to(meta_tokens[None], (n_prompt, N_META, D_MODEL)), x_prompt], axis=1)
    xp = xp.reshape(n_prompt * tp, D_MODEL)
    xs = x_sample.reshape(n_seq * n_new, D_MODEL)
    tab_p = _rotary_tables(jnp.arange(tp, dtype=I32) - pad)
    tab_s = _rotary_tables(past_len + (jnp.arange(ROW_TILE, dtype=I32) % n_new))
    valid_p = (jnp.arange(tp) >= pad).astype(F32)[:, None]
    valid_s = jnp.ones((ROW_TILE, 1), F32)
    valid_p_flat = jnp.tile(valid_p, (n_prompt, 1))
    pt_flat = page_table.reshape(-1).astype(I32)
    cache_a = cache_kv_a.reshape(depth, n_pool, page * 2 * KV_A, 2 * HD_A)
    cache_b_t = jnp.transpose(cache_kv_b, (0, 1, 3, 4, 5, 2)).reshape(depth, n_pool, 2 * KV_B * HD_B, page)
    cache_i_t = jnp.transpose(cache_idx_b, (0, 1, 3, 2))

    row1 = lambda a: a.reshape(1, -1)
    pad_seq = lambda a: jnp.pad(a.reshape(n_seq, n_new, -1), ((0, 0), (0, S_ROWS - n_new), (0, 0)))
    unpad_seq = lambda a: a[:, :n_new].reshape(n_seq * n_new, -1)
    per_seq = lambda a: a.reshape(n_prompt, tp, a.shape[-1])
    tiles_t = lambda a: a.reshape(n_prompt, tp // KEY_TILE, ROW_TILE, KEY_TILE)

    outs_p = [[] for _ in range(5)]
    outs_s = [[] for _ in range(5)]
    for l in range(depth):
        lam_init = 0.8 - 0.6 * math.exp(-0.3 * l)
        lam = (jnp.exp(jnp.sum(lam_q1[l] * lam_k1[l])) - jnp.exp(jnp.sum(lam_q2[l] * lam_k2[l]))
               + lam_init).reshape(1).astype(F32)
        out_scale = 1.0 - lam_init
        last = l == depth - 1
        w_in_l, w_in_t = _prep_w_in(w_in[l])
        wa, wb, wc, wo = (w_out_a[l].astype(BF16), w_out_b[l].astype(BF16), w_out_c[l].astype(BF16),
                          w_o[l].astype(BF16))
        wu, wd = w_up[l].astype(BF16), w_down[l].astype(BF16)
        ffn_w = (row1(norm_ffn[l]), wu, conv_f_w[l], row1(conv_f_b[l]), wd, row1(norm_final))
        conv_w = (conv_c_w[l], row1(conv_c_b[l]), row1(ln_c_g[l]), row1(ln_c_b[l]))

        z = _in_proj(xp, row1(norm_mix[l]), w_in_l, w_in_t, *tab_p, valid_p, tp // ROW_TILE)
        ya = _attn_a_prompt(lam, per_seq(z["aq"]), per_seq(z["kab"]),
                            z["avt"].reshape(n_prompt, tp // A_KEY_TILE, ROW_TILE, A_KEY_TILE),
                            subln_a[l].reshape(-1, 1), pad, out_scale)
        yb = _attn_b_prompt(per_seq(z["iq"]), z["iwt"], per_seq(z["bq"]), per_seq(z["ikkb"]), per_seq(z["bkk"]),
                            tiles_t(z["bvt"]), pad, k_sel_p)
        yc = _conv_c_prompt(z["u"], *conv_w)
        xp = _merge(ya.reshape(-1, 512), yb.reshape(-1, 512), yc, z["gz"], xp, wa, wb, wc, wo)
        xp, tails = _ffn(xp, *ffn_w, (valid_p_flat,), carried=True, final_norm=last)
        outs_p[0].append(per_seq(z["kva"])[:, pad:].reshape(n_prompt, t_len, 2, KV_A, 2 * HD_A))
        outs_p[1].append(per_seq(z["kvb"])[:, pad:].reshape(n_prompt, t_len, 2, KV_B, HD_B))
        outs_p[2].append(per_seq(z["ikk"])[:, pad:, :D_I])
        outs_p[3].append(per_seq(z["u"])[:, tp - (CONV_C - 1):])
        tails = tails.reshape(n_prompt, tp // ROW_TILE, TAIL, 2 * D_FF)
        outs_p[4].append(tails[:, -1, TAIL - (CONV_F - 1):])

        z = _in_proj(xs, row1(norm_mix[l]), w_in_l, w_in_t, *tab_s, valid_s, 1)
        ya = _attn_a_sample(pt_flat, lam, pad_seq(z["aq"]), pad_seq(z["kva"]), row1(subln_a[l]), cache_a, l,
                            n_new, out_scale)
        yb = _attn_b_sample(pt_flat, pad_seq(z["iq"]), pad_seq(z["iw"]), pad_seq(z["bq"]), pad_seq(z["kvb"]),
                            pad_seq(z["ikk"]), cache_b_t, cache_i_t, l, n_new, k_sel_s)
        u_seq = z["u"].reshape(n_seq, n_new, D_C)
        yc = _conv_c_sample(jnp.swapaxes(state_conv_c[l], 0, 1), jnp.swapaxes(u_seq, 0, 1), *conv_w)
        yc = jnp.swapaxes(yc, 0, 1).reshape(n_seq * n_new, D_C)
        xs = _merge(unpad_seq(ya), unpad_seq(yb), yc, z["gz"], xs, wa, wb, wc, wo)
        st_f = state_conv_ffn[l]
        zrow = jnp.zeros_like(st_f[:, :1])
        h1 = jnp.concatenate([st_f[:, 1:2]] + [zrow] * (n_new - 1), axis=1).reshape(n_seq * n_new, 2 * D_FF)
        h2 = jnp.concatenate([st_f[:, 0:1], st_f[:, 1:2]] + [zrow] * (n_new - 2), axis=1).reshape(
            n_seq * n_new, 2 * D_FF)
        xs, up = _ffn(xs, *ffn_w, (h1, h2), carried=False, final_norm=last, n_new=n_new)
        outs_s[0].append(z["kva"].reshape(n_seq, n_new, 2, KV_A, 2 * HD_A))
        outs_s[1].append(z["kvb"].reshape(n_seq, n_new, 2, KV_B, HD_B))
        outs_s[2].append(z["ikk"][:, :D_I].reshape(n_seq, n_new, D_I))
        outs_s[3].append(jnp.concatenate([state_conv_c[l], u_seq], axis=1)[:, -(CONV_C - 1):])
        up_ext = jnp.concatenate([st_f, up.reshape(n_seq, n_new, 2 * D_FF)], axis=1)
        outs_s[4].append(up_ext[:, -(CONV_F - 1):])

    y_prompt = xp.reshape(n_prompt, tp, D_MODEL)[:, pad + N_META:]
    y_sample = xs.reshape(n_seq, n_new, D_MODEL)
    stk = lambda xs_: jnp.stack(xs_, axis=0)
    return (y_prompt, y_sample, *[stk(o) for o in outs_p], *[stk(o) for o in outs_s])
```

```python
import functools
import math

import jax
import jax.numpy as jnp
from jax import lax
from jax.experimental import pallas as pl
from jax.experimental.pallas import tpu as pltpu

D_MODEL = 1024
N_META = 16
H_A, KV_A, HD_A = 4, 2, 64
G_A = H_A // KV_A
VD_A = 2 * HD_A
H_B, KV_B, HD_B = 8, 2, 64
G_B = H_B // KV_B
H_I, D_I = 8, 64
TOPK_MAX = 256
D_C, CONV_C = 512, 31
D_FF, CONV_F = 2816, 3
ROPE_THETA = 500000.0
ROT_HALF = 8
EPS = 1e-6
LN_EPS = 1e-5

LANE = 128
SUBLANE = 8
ROW_TILE = 256
Q_TILE = 128
KEY_TILE = 256
A_KEY_TILE = 256
FF_CHUNK = 256
FFN_TILE = 512
VMEM_LIMIT = 56 * 1024 * 1024

LOG2E = 1.4426950408889634
MASKED = -1e30
M_FLOOR = -5e29
NO_SCORE = -3.0e38
INT_MIN = -(2 ** 31)
INT_MAX = 2 ** 31 - 1
KEY_F_MAX = 2147483392.0
SEARCH_CAP = 80

F32 = jnp.float32
BF16 = jnp.bfloat16
I32 = jnp.int32

C_AQ, C_KVA, C_BQ, C_BKK, C_KVB, C_IQ, C_IKK, C_IW, C_CA, C_CG, C_GZ = (
    0, 512, 1024, 1536, 1792, 2048, 2560, 2688, 2816, 3328, 3840)
N_PROJ = C_GZ + 3 * D_MODEL
R_AVT, R_BVT, R_IWT, N_PROJ_T = 0, 256, 512, 528


def _params(sem):
    return pltpu.CompilerParams(dimension_semantics=sem, vmem_limit_bytes=VMEM_LIMIT)


def _const_spec(shape):
    nd = len(shape)
    return pl.BlockSpec(shape, lambda *_: (0,) * nd)


def _rms(x, g):
    return x * lax.rsqrt(jnp.mean(x * x, axis=-1, keepdims=True) + EPS) * g


def _nt_dot(a, b):
    return lax.dot_general(a, b, (((1,), (1,)), ((), ())), preferred_element_type=F32)


def _lane_ids(shape):
    return lax.broadcasted_iota(I32, shape, len(shape) - 1)


def _row_ids(shape):
    return lax.broadcasted_iota(I32, shape, 0)


def _in_proj_kernel(x_ref, g_ref, w_ref, wt_ref, cos_ref, sa_ref, sb_ref, valid_ref,
                    aq_ref, kva_ref, kab_ref, avt_ref, bq_ref, bkk_ref, bvt_ref, kvb_ref,
                    iq_ref, ikk_ref, ikkb_ref, iw_ref, iwt_ref, u_ref, gz_ref):
    xb = _rms(x_ref[...], g_ref[...]).astype(BF16)
    cos, sa, sb = cos_ref[...], sa_ref[...], sb_ref[...]

    def proj(start, width):
        return jnp.dot(xb, w_ref[:, start:start + width], preferred_element_type=F32)

    def proj_t(start, height):
        return _nt_dot(wt_ref[start:start + height, :], xb)

    def rot(z):
        return z * cos + pltpu.roll(z, LANE - ROT_HALF, 1) * sa + pltpu.roll(z, ROT_HALF, 1) * sb

    def groups(z, n_rot):
        n = z.shape[1] // LANE
        return [rot(z[:, k * LANE:(k + 1) * LANE]) if k < n_rot else z[:, k * LANE:(k + 1) * LANE]
                for k in range(n)]

    def put(ref, k, val):
        ref[:, k * LANE:(k + 1) * LANE] = val.astype(ref.dtype)

    for k, zk in enumerate(groups(proj(C_AQ, 512), 4)):
        put(aq_ref, k, zk * (HD_A ** -0.5 * LOG2E))
    for k, zk in enumerate(groups(proj(C_KVA, 512), 2)):
        put(kva_ref, k, zk)
        if k < 2:
            put(kab_ref, k, zk)
    avt = proj_t(R_AVT, 256).astype(BF16)
    for h in range(ROW_TILE // A_KEY_TILE):
        avt_ref[h] = avt[:, h * A_KEY_TILE:(h + 1) * A_KEY_TILE]
    for k, zk in enumerate(groups(proj(C_BQ, 512), 4)):
        put(bq_ref, k, zk * (HD_B ** -0.5 * LOG2E))
    for k, zk in enumerate(groups(proj(C_BKK, 256), 2)):
        put(bkk_ref, k, zk)
    bvt_ref[...] = proj_t(R_BVT, 256).astype(BF16)
    for k, zk in enumerate(groups(proj(C_KVB, 256), 1)):
        put(kvb_ref, k, zk)
    for k, zk in enumerate(groups(proj(C_IQ, 512), 4)):
        put(iq_ref, k, zk * (D_I ** -0.5))
    ikk = rot(proj(C_IKK, LANE))
    ikk_ref[...] = ikk
    ikkb_ref[...] = ikk.astype(BF16)
    iw_ref[...] = proj(C_IW, LANE) * (H_I ** -0.5)
    iwt_ref[...] = proj_t(R_IWT, 2 * SUBLANE)[:H_I] * (H_I ** -0.5)
    u_ref[...] = proj(C_CA, D_C) * jax.nn.sigmoid(proj(C_CG, D_C)) * valid_ref[...]
    gz_ref[...] = proj(C_GZ, 3 * D_MODEL)


IN_PROJ_OUTS = ("aq", "kva", "kab", "avt", "bq", "bkk", "bvt", "kvb", "iq", "ikk", "ikkb", "iw", "iwt", "u", "gz")


def _in_proj(x, g, w, wt, cos, sa, sb, valid, table_tiles):
    n = x.shape[0]
    nt = n // ROW_TILE
    row = lambda width: pl.BlockSpec((ROW_TILE, width), lambda i: (i, 0))
    tab = pl.BlockSpec((ROW_TILE, LANE), lambda i: (i % table_tiles, 0))
    rows = lambda width, dt: (row(width), jax.ShapeDtypeStruct((n, width), dt))
    tile_t = (pl.BlockSpec((None, ROW_TILE, ROW_TILE), lambda i: (i, 0, 0)),
              jax.ShapeDtypeStruct((nt, ROW_TILE, ROW_TILE), BF16))
    n_a = ROW_TILE // A_KEY_TILE
    tile_a = (pl.BlockSpec((None, n_a, ROW_TILE, A_KEY_TILE), lambda i: (i, 0, 0, 0)),
              jax.ShapeDtypeStruct((nt, n_a, ROW_TILE, A_KEY_TILE), BF16))
    outs = [rows(512, BF16), rows(512, F32), rows(256, BF16), tile_a, rows(512, BF16), rows(256, BF16), tile_t,
            rows(256, F32), rows(512, BF16), rows(LANE, F32), rows(LANE, BF16), rows(LANE, F32),
            (pl.BlockSpec((H_I, ROW_TILE), lambda i: (0, i)), jax.ShapeDtypeStruct((H_I, n), F32)),
            rows(D_C, F32), rows(3 * D_MODEL, F32)]
    res = pl.pallas_call(
        _in_proj_kernel,
        grid=(nt,),
        in_specs=[row(D_MODEL), _const_spec((1, D_MODEL)), _const_spec((D_MODEL, N_PROJ)),
                  _const_spec((N_PROJ_T, D_MODEL)),
                  tab, tab, tab, pl.BlockSpec((ROW_TILE, 1), lambda i: (i % table_tiles, 0))],
        out_specs=[o[0] for o in outs],
        out_shape=[o[1] for o in outs],
        compiler_params=_params(("parallel",)),
        name="in_proj",
    )(x, g, w, wt, cos, sa, sb, valid)
    return dict(zip(IN_PROJ_OUTS, res))


def _prep_w_in(w):
    o_aq, o_ak, o_av, o_bq, o_bk, o_bv, o_iq, o_iw, o_ik, o_cin, o_gz = (
        0, 512, 768, 1024, 1536, 1664, 1792, 2304, 2312, 2376, 3400)
    bk0, bk1 = w[:, o_bk:o_bk + 64], w[:, o_bk + 64:o_bk + 128]
    bv0, bv1 = w[:, o_bv:o_bv + 64], w[:, o_bv + 64:o_bv + 128]
    ik = w[:, o_ik:o_ik + D_I]
    iw = w[:, o_iw:o_iw + H_I]
    cols = [w[:, o_aq:o_aq + 512], w[:, o_ak:o_ak + 256], w[:, o_av:o_av + 256], w[:, o_bq:o_bq + 512],
            bk0, bk0, bk1, bk1,
            w[:, o_bk:o_bk + 128], w[:, o_bv:o_bv + 128],
            w[:, o_iq:o_iq + 512], ik, ik,
            iw, jnp.zeros((w.shape[0], LANE - H_I), w.dtype),
            w[:, o_cin:o_cin + 2 * D_C], w[:, o_gz:o_gz + 3 * D_MODEL]]
    out = jnp.concatenate(cols, axis=1).astype(BF16)
    cols_t = [w[:, o_av:o_av + 256], bv0, bv0, bv1, bv1, iw,
              jnp.zeros((w.shape[0], N_PROJ_T - R_IWT - H_I), w.dtype)]
    out_t = jnp.concatenate(cols_t, axis=1).T.astype(BF16)
    assert out.shape[1] == N_PROJ and out_t.shape[0] == N_PROJ_T
    return out, out_t


def _rotary_tables(pos):
    rot = 2 * ROT_HALF
    inv = jnp.power(ROPE_THETA, -jnp.arange(ROT_HALF, dtype=F32) * 2.0 / rot)
    ang = pos.astype(F32)[:, None] * inv[None, :]
    cos, sin = jnp.cos(ang), jnp.sin(ang)
    t = pos.shape[0]
    rest = 64 - rot
    one = jnp.ones((t, rest), F32)
    zero = jnp.zeros((t, rest), F32)
    zh = jnp.zeros((t, ROT_HALF), F32)
    c64 = jnp.concatenate([cos, cos, one], axis=1)
    sa64 = jnp.concatenate([-sin, zh, zero], axis=1)
    sb64 = jnp.concatenate([zh, sin, zero], axis=1)
    dup = lambda a: jnp.concatenate([a, a], axis=1)
    return dup(c64), dup(sa64), dup(sb64)


def _split_pair(pair):
    lo = _lane_ids(pair.shape) < 64
    zero = jnp.zeros_like(pair)
    return jnp.where(lo, pair, zero), jnp.where(lo, zero, pair)


def _monotone_key(score):
    b = lax.bitcast_convert_type(score, I32)
    return jnp.where(b < 0, b ^ I32(0x7FFFFFFF), b)


Q_PAIR = 2 * Q_TILE


def _pair_up(q_groups):
    return [jnp.concatenate(q_groups[n:n + 2], axis=0) for n in range(0, len(q_groups), 2)]


def _two_tile_pipeline(prefetch, consume, state, j_first, j_last, mask_middle):
    n_trips = (j_last - j_first + 2) // 2
    carry = (prefetch(0, j_first, True), state)
    segments = ((0, 1, True), (1, n_trips - 2, mask_middle), (jnp.maximum(n_trips - 2, 1), n_trips, True))
    for first, stop, masked in segments:
        def body(jj, c, masked=masked):
            t0 = j_first + 2 * jj
            aux_b = prefetch(1, t0 + 1, masked)
            st = consume(0, t0, c[0], c[1], masked)
            aux_a = prefetch(0, t0 + 2, masked)
            return aux_a, consume(1, t0 + 1, aux_b, st, masked)
        carry = lax.fori_loop(first, stop, body, carry)
    return carry[1]


def _flash_run_t(load_k, load_vt, bias_of, q_pairs, j_first, j_last, mask_middle, s_ref):
    kv_of = [kv for kv, pairs in enumerate(q_pairs) for _ in pairs]
    qs = [q for pairs in q_pairs for q in pairs]

    def prefetch(slot, j, masked):
        jc = jnp.minimum(j, j_last)
        if masked:
            bias = bias_of(j)
            bias = jnp.concatenate([bias, bias], axis=1)
        k = [load_k(jc, kv) for kv in range(len(q_pairs))]
        cmax = []
        for n, (kv, q) in enumerate(zip(kv_of, qs)):
            s = _nt_dot(k[kv], q)
            if masked:
                s = s + bias
            s_ref[slot, n] = s
            cmax.append(jnp.max(s, axis=0, keepdims=True))
        return cmax

    def step(slot, j, cmax, state, masked):
        del masked
        jc = jnp.minimum(j, j_last)
        v_t = [load_vt(jc, kv) for kv in range(len(q_pairs))]
        out = []
        for n, (kv, (m, l, acc)) in enumerate(zip(kv_of, state)):
            m_new = jnp.maximum(m, cmax[n])
            alpha = jnp.exp2(m - m_new)
            p = jnp.exp2(s_ref[slot, n] - m_new)
            l = alpha * l + jnp.sum(p, axis=0, keepdims=True)
            acc = alpha * acc + jnp.dot(v_t[kv], p.astype(BF16), preferred_element_type=F32)
            out.append((m_new, l, acc))
        return out

    state = [(jnp.full((1, Q_PAIR), M_FLOOR, F32), jnp.zeros((1, Q_PAIR), F32), jnp.zeros((LANE, Q_PAIR), F32))
             for _ in kv_of]
    state = _two_tile_pipeline(prefetch, step, state, j_first, j_last, mask_middle)
    return [acc / jnp.where(l > 0.0, l, 1.0) for _, l, acc in state]


def _attn_a_kernel(lam_ref, q_ref, k_ref, vt_ref, sub_ref, o_ref, s_ref, *, pad, out_scale):
    i = pl.program_id(1)
    q0 = i * Q_TILE

    @pl.when(q0 + Q_TILE <= pad)
    def _():
        o_ref[...] = jnp.zeros_like(o_ref)

    @pl.when(q0 + Q_TILE > pad)
    def _():
        lam = lam_ref[0]
        kt = A_KEY_TILE
        j_pad = pad // kt
        j_diag = (q0 + Q_TILE - 1) // kt
        qpos = _lane_ids((kt, Q_TILE)) + q0
        krow = _row_ids((kt, Q_TILE))
        q_pairs = []
        for kv in range(KV_A):
            q_groups = []
            for g in range(G_A):
                c0 = (kv * G_A + g) * LANE
                q_groups += list(_split_pair(q_ref[:, c0:c0 + LANE]))
            q_pairs.append(_pair_up(q_groups))

        def load_k(j, kv):
            return k_ref[pl.ds(pl.multiple_of(j * kt, kt), kt), kv * LANE:(kv + 1) * LANE]

        def load_vt(j, kv):
            return vt_ref[j, kv * LANE:(kv + 1) * LANE, :]

        def bias_of(j):
            kpos = krow + j * kt
            return jnp.where((kpos <= qpos) & (kpos >= pad), 0.0, MASKED)

        outs = _flash_run_t(load_k, load_vt, bias_of, q_pairs, j_pad, j_diag, False, s_ref)
        for kv in range(KV_A):
            o = outs[kv * G_A:(kv + 1) * G_A]
            for g in range(G_A):
                y = o[g][:, :Q_TILE] - lam * o[g][:, Q_TILE:]
                y = y * lax.rsqrt(jnp.mean(y * y, axis=0, keepdims=True) + EPS) * sub_ref[...] * out_scale
                c0 = (kv * G_A + g) * LANE
                o_ref[:, c0:c0 + LANE] = y.T.astype(o_ref.dtype)


def _attn_a_prompt(lam, aq, kab, avt, subln_col, pad, out_scale):
    b, tp, _ = aq.shape
    return pl.pallas_call(
        functools.partial(_attn_a_kernel, pad=pad, out_scale=out_scale),
        grid=(b, tp // Q_TILE),
        in_specs=[pl.BlockSpec(memory_space=pltpu.SMEM),
                  pl.BlockSpec((None, Q_TILE, 512), lambda bi, i: (bi, i, 0)),
                  pl.BlockSpec((None, tp, 256), lambda bi, i: (bi, 0, 0)),
                  pl.BlockSpec((None, tp // A_KEY_TILE, 256, A_KEY_TILE), lambda bi, i: (bi, 0, 0, 0)),
                  _const_spec((LANE, 1))],
        out_specs=pl.BlockSpec((None, Q_TILE, 512), lambda bi, i: (bi, i, 0)),
        out_shape=jax.ShapeDtypeStruct((b, tp, 512), BF16),
        scratch_shapes=[pltpu.VMEM((2, KV_A * G_A, A_KEY_TILE, Q_PAIR), F32)],
        compiler_params=_params(("parallel", "parallel")),
        name="attn_a_prompt",
    )(lam, aq, kab, avt, subln_col)


def _kth_largest_key(count_ge, k_sel, n_vis, score_min, score_max, passes_per_check):
    kf = float(k_sel)
    key_min, key_max = _monotone_key(score_min), _monotone_key(score_max)
    no_key = _monotone_key(jnp.full(n_vis.shape, NO_SCORE, F32))
    zero = jnp.zeros_like(key_min)
    one = zero + 1
    c_pos = count_ge(one)
    c_nn = count_ge(zero)
    few = n_vis <= kf
    pos = c_pos >= kf
    neg = c_nn < kf
    lo = jnp.where(pos, one, jnp.where(neg, key_min, zero))
    hi = jnp.where(pos, key_max + 1, jnp.where(neg, zero, one))
    c_lo = jnp.where(pos, c_pos, jnp.where(neg, n_vis, c_nn))
    c_hi = jnp.where(pos, 0.0, jnp.where(neg, c_nn, c_pos))
    lo = jnp.where(few, no_key + 1, lo)
    done = few | (c_lo == kf) | (hi - 1 <= lo)
    ones = jnp.ones_like(n_vis)

    def cond(st):
        it, active = st[0], st[-1]
        return (it < SEARCH_CAP) & (jnp.sum(active) > 0.0)

    def body(st):
        it, lo, hi, c_lo, c_hi, f_lo, f_hi, last, active = st
        done = active == 0.0
        g_lo = (c_lo - (kf - 0.5)) * f_lo
        g_hi = ((kf - 0.5) - c_hi) * f_hi
        frac = jnp.where(done, 0.5, g_hi / (g_lo + g_hi))
        lo_f, hi_f = lo.astype(F32), hi.astype(F32)
        t_f = jnp.clip(hi_f - (hi_f - lo_f) * frac, -KEY_F_MAX, KEY_F_MAX)
        t_mid = (lo >> 1) + (hi >> 1) + (lo & hi & 1)
        halve = (f_lo < 0.2) | (f_hi < 0.2) | (c_lo - c_hi <= 4.0)
        t = jnp.where(halve, t_mid, t_f.astype(I32))
        t = jnp.minimum(jnp.maximum(t, lo + 1), hi - 1)
        c = count_ge(t)
        up = c >= kf
        act_up = jnp.logical_and(jnp.logical_not(done), up)
        act_dn = jnp.logical_and(jnp.logical_not(done), jnp.logical_not(up))
        f_lo = jnp.where(act_dn & (last == -1), f_lo * 0.5, jnp.where(act_up, ones, f_lo))
        f_hi = jnp.where(act_up & (last == 1), f_hi * 0.5, jnp.where(act_dn, ones, f_hi))
        last = jnp.where(act_up, 1, jnp.where(act_dn, -1, last))
        lo = jnp.where(act_up, t, lo)
        c_lo = jnp.where(act_up, c, c_lo)
        hi = jnp.where(act_dn, t, hi)
        c_hi = jnp.where(act_dn, c, c_hi)
        done = done | (c_lo == kf) | (hi - 1 <= lo)
        return it + 1, lo, hi, c_lo, c_hi, f_lo, f_hi, last, jnp.where(done, 0.0, 1.0)

    def trip(s):
        for _ in range(passes_per_check):
            s = body(s)
        return s

    st = lax.while_loop(cond, trip,
                        (I32(0), lo, hi, c_lo, c_hi, ones, ones, zero, jnp.where(done, 0.0, 1.0)))
    return st[1], st[3]


def _tie_cut(count_tied_before, n_keep, n_pos_bits):
    def step(it, x):
        cand = x + lax.shift_left(I32(1), I32(n_pos_bits - 1) - it)
        return jnp.where(count_tied_before(cand) < n_keep, cand, x)

    return lax.fori_loop(0, n_pos_bits, step, jnp.zeros(n_keep.shape, I32))


def _attn_b_kernel(iq_ref, iwt_ref, q_ref, ik_ref, kk_ref, vvt_ref, o_ref, key_ref, s_ref, d_ref, *, pad, k_sel):
    i = pl.program_id(1)
    q0 = i * Q_TILE

    @pl.when(q0 + Q_TILE <= pad)
    def _():
        o_ref[...] = jnp.zeros_like(o_ref)

    @pl.when(q0 + Q_TILE > pad)
    def _():
        j_diag = (q0 + Q_TILE - 1) // KEY_TILE
        n_kb = j_diag + 1
        qpos = _lane_ids((KEY_TILE, Q_TILE)) + q0
        krow = _row_ids((KEY_TILE, Q_TILE))
        no_key = _monotone_key(jnp.full((1, 1), NO_SCORE, F32))
        def fold(a, op):
            parts = [a[r * SUBLANE:(r + 1) * SUBLANE] for r in range(KEY_TILE // SUBLANE)]
            while len(parts) > 1:
                parts = [op(parts[n], parts[n + 1]) for n in range(0, len(parts), 2)]
            return parts[0]

        parts = []
        for pr in range(H_I // 2):
            parts += list(_split_pair(iq_ref[:, pr * LANE:(pr + 1) * LANE]))
        qi = jnp.concatenate(parts, axis=0)
        iwt = iwt_ref[...]

        def dots(slot, j, masked):
            k0 = pl.multiple_of(jnp.minimum(j, j_diag) * KEY_TILE, KEY_TILE)
            d_ref[slot] = _nt_dot(ik_ref[pl.ds(k0, KEY_TILE), :], qi)
            return ()

        def score_tile(slot, j, _, carry, masked):
            kmin, kmax = carry
            sc = jnp.zeros((KEY_TILE, Q_TILE), F32)
            for h in range(H_I):
                sc = sc + iwt[h:h + 1, :] * jnp.maximum(d_ref[slot, :, h * Q_TILE:(h + 1) * Q_TILE], 0.0)
            sc = sc + 0.0
            sc_lo = sc_hi = sc
            key = _monotone_key(sc)
            if masked:
                kpos = krow + j * KEY_TILE
                vis = (kpos <= qpos) & (kpos >= pad)
                key = jnp.where(vis, key, no_key)
                sc_lo = jnp.where(vis, sc, -NO_SCORE)
                sc_hi = jnp.where(vis, sc, NO_SCORE)
            key_ref[j] = key
            return jnp.minimum(kmin, fold(sc_lo, jnp.minimum)), jnp.maximum(kmax, fold(sc_hi, jnp.maximum))

        carry = (jnp.full((SUBLANE, Q_TILE), -NO_SCORE, F32), jnp.full((SUBLANE, Q_TILE), NO_SCORE, F32))
        carry = _two_tile_pipeline(dots, score_tile, carry, 0, j_diag, False)
        sc_min = jnp.min(carry[0], axis=0, keepdims=True)
        sc_max = jnp.max(carry[1], axis=0, keepdims=True)

        key_ref[n_kb] = jnp.broadcast_to(no_key, (KEY_TILE, Q_TILE))

        def count_ge(t):
            def cbody(jj, c):
                for h in range(2):
                    c = c + fold(jnp.where(key_ref[2 * jj + h] >= t, 1.0, 0.0), jnp.add)
                return c
            c = lax.fori_loop(0, (n_kb + 1) // 2, cbody, jnp.zeros((SUBLANE, Q_TILE), F32))
            return jnp.sum(c, axis=0, keepdims=True)

        n_vis = (_lane_ids((1, Q_TILE)) + (q0 - pad + 1)).astype(F32)
        thr, c_thr = _kth_largest_key(count_ge, k_sel, n_vis, sc_min, sc_max, 2)
        tied = c_thr > float(k_sel)

        @pl.when(jnp.sum(jnp.where(tied, 1.0, 0.0)) > 0.0)
        def _():
            n_keep = float(k_sel) - count_ge(thr + 1)

            def tied_before(x):
                def cbody(j, c):
                    hit = (key_ref[j] == thr) & (krow + j * KEY_TILE < x)
                    return c + fold(jnp.where(hit, 1.0, 0.0), jnp.add)
                c = lax.fori_loop(0, n_kb, cbody, jnp.zeros((SUBLANE, Q_TILE), F32))
                return jnp.sum(c, axis=0, keepdims=True)

            x = _tie_cut(tied_before, n_keep, (key_ref.shape[0] * KEY_TILE).bit_length())

            def lower(j, _):
                key = key_ref[j]
                drop = tied & (key == thr) & (krow + j * KEY_TILE > x)
                key_ref[j] = jnp.where(drop, key - 1, key)
                return 0

            lax.fori_loop(0, n_kb, lower, 0)

        q_pairs = []
        for kv in range(KV_B):
            q_groups = []
            for pr in range(G_B // 2):
                c0 = (kv * (G_B // 2) + pr) * LANE
                q_groups += list(_split_pair(q_ref[:, c0:c0 + LANE]))
            q_pairs.append(_pair_up(q_groups))

        def load_k(j, kv):
            return kk_ref[pl.ds(pl.multiple_of(j * KEY_TILE, KEY_TILE), KEY_TILE), kv * LANE:(kv + 1) * LANE]

        def load_vt(j, kv):
            return vvt_ref[j, kv * LANE:(kv + 1) * LANE, :]

        def bias_of(j):
            thr_j = jnp.where(j <= j_diag, thr, INT_MAX)
            return jnp.where(key_ref[jnp.minimum(j, j_diag)] >= thr_j, 0.0, MASKED)

        outs = _flash_run_t(load_k, load_vt, bias_of, q_pairs, 0, j_diag, True, s_ref)
        lo = _lane_ids((Q_TILE, LANE)) < 64
        for kv in range(KV_B):
            o = outs[kv * (G_B // 2):(kv + 1) * (G_B // 2)]
            for pr in range(G_B // 2):
                pair = jnp.where(lo, o[pr][:, :Q_TILE].T, o[pr][:, Q_TILE:].T)
                c0 = (kv * (G_B // 2) + pr) * LANE
                o_ref[:, c0:c0 + LANE] = pair.astype(o_ref.dtype)


def _attn_b_prompt(iq, iwt, bq, ikkb, bkk, bvt, pad, k_sel):
    b, tp, _ = bq.shape
    nq = tp // Q_TILE
    qspec = lambda width: pl.BlockSpec((None, Q_TILE, width), lambda bi, i: (bi, i, 0))
    full = lambda width: pl.BlockSpec((None, tp, width), lambda bi, i: (bi, 0, 0))
    return pl.pallas_call(
        functools.partial(_attn_b_kernel, pad=pad, k_sel=k_sel),
        grid=(b, nq),
        in_specs=[qspec(512), pl.BlockSpec((H_I, Q_TILE), lambda bi, i: (0, bi * nq + i)), qspec(512),
                  full(LANE), full(256),
                  pl.BlockSpec((None, tp // KEY_TILE, 256, KEY_TILE), lambda bi, i: (bi, 0, 0, 0))],
        out_specs=qspec(512),
        out_shape=jax.ShapeDtypeStruct((b, tp, 512), BF16),
        scratch_shapes=[pltpu.VMEM((tp // KEY_TILE + 1, KEY_TILE, Q_TILE), I32),
                        pltpu.VMEM((2, KV_B * G_B // 2, KEY_TILE, Q_PAIR), F32),
                        pltpu.VMEM((2, KEY_TILE, H_I * Q_TILE), F32)],
        compiler_params=_params(("parallel", "parallel")),
        name="attn_b_prompt",
    )(iq, iwt, bq, ikkb, bkk, bvt)


def _ln_swish(y, g, b):
    mu = jnp.mean(y, axis=-1, keepdims=True)
    yc = y - mu
    z = yc * lax.rsqrt(jnp.mean(yc * yc, axis=-1, keepdims=True) + LN_EPS) * g + b
    return z * jax.nn.sigmoid(z)


HALO = 32


def _conv_c_prompt_kernel(prev_ref, cur_ref, w_ref, b_ref, g_ref, beta_ref, o_ref, ext_ref):
    ext_ref[0:HALO, :] = prev_ref[ROW_TILE - HALO:ROW_TILE, :]
    ext_ref[HALO:HALO + ROW_TILE, :] = cur_ref[...]
    acc = jnp.zeros((ROW_TILE, D_C), F32) + b_ref[...]
    base = HALO - (CONV_C - 1)
    for j in range(CONV_C):
        acc = acc + w_ref[j:j + 1, :] * ext_ref[base + j:base + j + ROW_TILE, :]
    o_ref[...] = _ln_swish(acc, g_ref[...], beta_ref[...]).astype(o_ref.dtype)


def _conv_c_prompt(u, w, b, g, beta):
    n = u.shape[0]
    return pl.pallas_call(
        _conv_c_prompt_kernel,
        grid=(n // ROW_TILE,),
        in_specs=[pl.BlockSpec((ROW_TILE, D_C), lambda i: (jnp.maximum(i - 1, 0), 0)),
                  pl.BlockSpec((ROW_TILE, D_C), lambda i: (i, 0)),
                  _const_spec((CONV_C, D_C)), _const_spec((1, D_C)), _const_spec((1, D_C)),
                  _const_spec((1, D_C))],
        out_specs=pl.BlockSpec((ROW_TILE, D_C), lambda i: (i, 0)),
        out_shape=jax.ShapeDtypeStruct((n, D_C), BF16),
        scratch_shapes=[pltpu.VMEM((HALO + ROW_TILE, D_C), F32)],
        compiler_params=_params(("parallel",)),
        name="conv_c_prompt",
    )(u, u, w, b, g, beta)


def _conv_c_sample_kernel(st_ref, u_ref, w_ref, b_ref, g_ref, beta_ref, o_ref):
    n_hist = st_ref.shape[0]
    for t in range(u_ref.shape[0]):
        acc = jnp.zeros(o_ref.shape[1:], F32) + b_ref[...]
        for j in range(CONV_C):
            e = t + j
            row = st_ref[e] if e < n_hist else u_ref[e - n_hist]
            acc = acc + w_ref[j:j + 1, :] * row
        o_ref[t] = _ln_swish(acc, g_ref[...], beta_ref[...]).astype(o_ref.dtype)


def _conv_c_sample(st_t, u_t, w, b, g, beta):
    n_new, n_seq, _ = u_t.shape
    return pl.pallas_call(
        _conv_c_sample_kernel,
        grid=(1,),
        in_specs=[_const_spec(st_t.shape), _const_spec(u_t.shape), _const_spec((CONV_C, D_C)),
                  _const_spec((1, D_C)), _const_spec((1, D_C)), _const_spec((1, D_C))],
        out_specs=_const_spec((n_new, n_seq, D_C)),
        out_shape=jax.ShapeDtypeStruct((n_new, n_seq, D_C), BF16),
        compiler_params=_params(("arbitrary",)),
        name="conv_c_sample",
    )(st_t, u_t, w, b, g, beta)


def _merge_kernel(ya_ref, yb_ref, yc_ref, gz_ref, x_ref, wa_ref, wb_ref, wc_ref, wo_ref, o_ref):
    merged = jnp.zeros((ROW_TILE, D_MODEL), F32)
    for k, (y_ref, w_ref) in enumerate(((ya_ref, wa_ref), (yb_ref, wb_ref), (yc_ref, wc_ref))):
        gate = jax.nn.sigmoid(gz_ref[:, k * D_MODEL:(k + 1) * D_MODEL])
        merged = merged + gate * jnp.dot(y_ref[...], w_ref[...], preferred_element_type=F32)
    o_ref[...] = x_ref[...] + jnp.dot(merged.astype(BF16), wo_ref[...], preferred_element_type=F32)


def _merge(ya, yb, yc, gz, x, wa, wb, wc, wo):
    n = x.shape[0]
    row = lambda width: pl.BlockSpec((ROW_TILE, width), lambda i: (i, 0))
    return pl.pallas_call(
        _merge_kernel,
        grid=(n // ROW_TILE,),
        in_specs=[row(512), row(512), row(D_C), row(3 * D_MODEL), row(D_MODEL),
                  _const_spec(wa.shape), _const_spec(wb.shape), _const_spec(wc.shape), _const_spec(wo.shape)],
        out_specs=row(D_MODEL),
        out_shape=jax.ShapeDtypeStruct((n, D_MODEL), F32),
        compiler_params=_params(("parallel",)),
        name="merge",
    )(ya, yb, yc, gz, x, wa, wb, wc, wo)


TAIL = 8


def _ffn_kernel(*refs, carried, final_norm, n_new):
    if carried:
        (x_ref, g_ref, wu_ref, cw_ref, cb_ref, wd_ref, gf_ref, valid_ref,
         o_ref, tail_ref, carry_ref) = refs
    else:
        (x_ref, g_ref, wu_ref, cw_ref, cb_ref, wd_ref, gf_ref, h1_ref, h2_ref,
         o_ref, up_ref) = refs
    x = x_ref[...]
    tm = x.shape[0]
    xb = _rms(x, g_ref[...]).astype(BF16)
    row = _row_ids((tm, 1))

    if carried:
        @pl.when(pl.program_id(0) == 0)
        def _():
            carry_ref[...] = jnp.zeros_like(carry_ref)
        valid = valid_ref[...]
    else:
        t_in_seq = row % n_new

    def conv(c0):
        up = jnp.dot(xb, wu_ref[:, c0:c0 + FF_CHUNK], preferred_element_type=F32)
        if carried:
            up = up * valid
            old = carry_ref[:, c0:c0 + FF_CHUNK]
            p1 = jnp.where(row == 0, old[TAIL - 1:TAIL], pltpu.roll(up, 1, 0))
            p2 = jnp.where(row == 0, old[TAIL - 2:TAIL - 1],
                           jnp.where(row == 1, old[TAIL - 1:TAIL], pltpu.roll(up, 2, 0)))
            carry_ref[:, c0:c0 + FF_CHUNK] = up[tm - TAIL:]
            for r in range(tm // ROW_TILE):
                tail_ref[r, :, c0:c0 + FF_CHUNK] = up[(r + 1) * ROW_TILE - TAIL:(r + 1) * ROW_TILE]
        else:
            p1 = jnp.where(t_in_seq < 1, h1_ref[:, c0:c0 + FF_CHUNK], pltpu.roll(up, 1, 0))
            p2 = jnp.where(t_in_seq < 2, h2_ref[:, c0:c0 + FF_CHUNK], pltpu.roll(up, 2, 0))
            up_ref[:, c0:c0 + FF_CHUNK] = up
        w = cw_ref[:, c0:c0 + FF_CHUNK]
        return w[0:1] * p2 + w[1:2] * p1 + w[2:3] * up + cb_ref[:, c0:c0 + FF_CHUNK]

    acc = jnp.zeros((tm, D_MODEL), F32)
    for c in range(D_FF // FF_CHUNK):
        hg = conv(c * FF_CHUNK)
        hv = conv(D_FF + c * FF_CHUNK)
        h = (hg * jax.nn.sigmoid(hg) * hv).astype(BF16)
        acc = acc + jnp.dot(h, wd_ref[c * FF_CHUNK:(c + 1) * FF_CHUNK, :], preferred_element_type=F32)
    y = x + acc
    o_ref[...] = _rms(y, gf_ref[...]) if final_norm else y


def _ffn(x, g, wu, cw, cb, wd, gf, extra, carried, final_norm, n_new=1):
    n = x.shape[0]
    tm = FFN_TILE if carried and n % FFN_TILE == 0 else ROW_TILE
    nt = n // tm
    row = lambda width: pl.BlockSpec((tm, width), lambda i: (i, 0))
    in_specs = [row(D_MODEL), _const_spec((1, D_MODEL)), _const_spec(wu.shape), _const_spec(cw.shape),
                _const_spec(cb.shape), _const_spec(wd.shape), _const_spec((1, D_MODEL))]
    if carried:
        halves = tm // ROW_TILE
        in_specs += [pl.BlockSpec((tm, 1), lambda i: (i, 0))]
        out_specs = [row(D_MODEL), pl.BlockSpec((None, halves, TAIL, 2 * D_FF), lambda i: (i, 0, 0, 0))]
        out_shape = [jax.ShapeDtypeStruct((n, D_MODEL), F32),
                     jax.ShapeDtypeStruct((nt, halves, TAIL, 2 * D_FF), F32)]
        scratch = [pltpu.VMEM((TAIL, 2 * D_FF), F32)]
    else:
        in_specs += [row(2 * D_FF), row(2 * D_FF)]
        out_specs = [row(D_MODEL), row(2 * D_FF)]
        out_shape = [jax.ShapeDtypeStruct((n, D_MODEL), F32), jax.ShapeDtypeStruct((n, 2 * D_FF), F32)]
        scratch = []
    return pl.pallas_call(
        functools.partial(_ffn_kernel, carried=carried, final_norm=final_norm, n_new=n_new),
        grid=(nt,),
        in_specs=in_specs, out_specs=out_specs, out_shape=out_shape, scratch_shapes=scratch,
        compiler_params=_params(("arbitrary",)),
        name="ffn_prompt" if carried else "ffn_sample",
    )(x, g, wu, cw, cb, wd, gf, *extra)


S_ROWS = 8


def _new_key_mask(shape, n_past, n_new):
    col = _lane_ids(shape)
    t = _row_ids(shape) % S_ROWS
    return (col < n_past) | ((col - n_past <= t) & (col < n_past + n_new))


def _pad_rows(a, rows):
    return jnp.concatenate([a, jnp.zeros((rows - a.shape[0], a.shape[1]), a.dtype)], axis=0)


def _page_specs(block, layer, n_pages):
    nd = len(block)
    return [pl.BlockSpec((None, None) + block,
                         lambda s, pt, j=j: (layer, pt[s * n_pages + j]) + (0,) * nd) for j in range(n_pages)]


def _attn_a_sample_kernel(pt_ref, lam_ref, q_ref, new_ref, sub_ref, *rest, n_pages, page, n_new, out_scale):
    page_refs, o_ref = rest[:n_pages], rest[n_pages]
    lam = lam_ref[0]
    n_past = n_pages * page
    q = q_ref[...].astype(F32)
    new = _pad_rows(new_ref[...], LANE)
    for kv in range(KV_A):
        parts = []
        for g in range(G_A):
            c0 = (kv * G_A + g) * LANE
            parts += list(_split_pair(q[:, c0:c0 + LANE]))
        qs = jnp.concatenate(parts, axis=0).astype(BF16)
        ksl = slice(kv * LANE, (kv + 1) * LANE)
        vsl = slice((KV_A + kv) * LANE, (KV_A + kv + 1) * LANE)
        k = jnp.concatenate([r[pl.ds(kv, page, stride=2 * KV_A), :].astype(BF16) for r in page_refs]
                            + [new[:, ksl].astype(BF16)], axis=0)
        v = jnp.concatenate([r[pl.ds(KV_A + kv, page, stride=2 * KV_A), :].astype(BF16) for r in page_refs]
                            + [new[:, vsl].astype(BF16)], axis=0)
        s = _nt_dot(qs, k)
        s = jnp.where(_new_key_mask(s.shape, n_past, n_new), s, MASKED)
        p = jnp.exp2(s - jnp.max(s, axis=1, keepdims=True))
        o = jnp.dot(p.astype(BF16), v, preferred_element_type=F32) / jnp.sum(p, axis=1, keepdims=True)
        for g in range(G_A):
            b0 = g * 2 * S_ROWS
            y = o[b0:b0 + S_ROWS] - lam * o[b0 + S_ROWS:b0 + 2 * S_ROWS]
            y = _rms(y, sub_ref[...]) * out_scale
            c0 = (kv * G_A + g) * LANE
            o_ref[:, c0:c0 + LANE] = y.astype(o_ref.dtype)


def _attn_a_sample(pt, lam, aq, new, subln, cache, layer, n_new, out_scale):
    n_seq = aq.shape[0]
    n_pages = pt.shape[0] // n_seq
    page = cache.shape[2] // (2 * KV_A)
    seq = lambda width: pl.BlockSpec((None, S_ROWS, width), lambda s, pt: (s, 0, 0))
    grid_spec = pltpu.PrefetchScalarGridSpec(
        num_scalar_prefetch=1,
        grid=(n_seq,),
        in_specs=[pl.BlockSpec(memory_space=pltpu.SMEM), seq(512), seq(512),
                  pl.BlockSpec((1, LANE), lambda s, pt: (0, 0))]
        + _page_specs((page * 2 * KV_A, LANE), layer, n_pages),
        out_specs=seq(512),
    )
    return pl.pallas_call(
        functools.partial(_attn_a_sample_kernel, n_pages=n_pages, page=page, n_new=n_new, out_scale=out_scale),
        grid_spec=grid_spec,
        out_shape=jax.ShapeDtypeStruct((n_seq, S_ROWS, 512), BF16),
        compiler_params=_params(("parallel",)),
        name="attn_a_sample",
    )(pt, lam, aq, new, subln, *([cache] * n_pages))


def _attn_b_sample_kernel(pt_ref, iq_ref, iw_ref, q_ref, newkv_ref, newik_ref, *rest,
                          n_pages, page, n_new, k_sel):
    kv_refs, ik_refs, o_ref = rest[:n_pages], rest[n_pages:2 * n_pages], rest[2 * n_pages]
    n_past = n_pages * page
    lo = _lane_ids((S_ROWS, LANE)) < 64

    def heads64(x):
        out = []
        for pr in range(x.shape[1] // LANE):
            pair = x[:, pr * LANE:(pr + 1) * LANE]
            out += [pair[:, :64], pltpu.roll(pair, 64, 1)[:, :64]]
        return out

    qi = jnp.concatenate(heads64(iq_ref[...].astype(F32)), axis=0).astype(BF16)
    ik_t = jnp.concatenate([r[...].astype(BF16) for r in ik_refs], axis=1)
    newik = _pad_rows(newik_ref[:, :D_I], LANE).astype(BF16)
    d = jnp.concatenate([jnp.dot(qi, ik_t, preferred_element_type=F32), _nt_dot(qi, newik)], axis=1)
    d = jnp.maximum(d, 0.0)
    iw = iw_ref[...]
    sc = jnp.zeros((S_ROWS, d.shape[1]), F32)
    for h in range(H_I):
        sc = sc + iw[:, h:h + 1] * d[h * S_ROWS:(h + 1) * S_ROWS]
    sc = sc + 0.0
    vis = _new_key_mask(sc.shape, n_past, n_new)
    no_key = _monotone_key(jnp.full((1, 1), NO_SCORE, F32))
    keys = jnp.where(vis, _monotone_key(sc), no_key)

    def count_ge(t):
        return jnp.sum(jnp.where(keys >= t, 1.0, 0.0), axis=1, keepdims=True)

    n_vis = jnp.sum(jnp.where(vis, 1.0, 0.0), axis=1, keepdims=True)
    sc_min = jnp.min(jnp.where(vis, sc, -NO_SCORE), axis=1, keepdims=True)
    sc_max = jnp.max(jnp.where(vis, sc, NO_SCORE), axis=1, keepdims=True)
    thr, c_thr = _kth_largest_key(count_ge, k_sel, n_vis, sc_min, sc_max, 1)
    tied = c_thr > float(k_sel)

    def break_ties(keys):
        n_keep = float(k_sel) - count_ge(thr + 1)
        kpos = _lane_ids(keys.shape)

        def tied_before(x):
            return jnp.sum(jnp.where((keys == thr) & (kpos < x), 1.0, 0.0), axis=1, keepdims=True)

        x = _tie_cut(tied_before, n_keep, keys.shape[1].bit_length())
        return jnp.where(tied & (keys == thr) & (kpos > x), keys - 1, keys)

    keys = lax.cond(jnp.sum(jnp.where(tied, 1.0, 0.0)) > 0.0, break_ties, lambda a: a, keys)
    bias = jnp.concatenate([jnp.where(keys >= thr, 0.0, MASKED)] * G_B, axis=0)

    q_heads = heads64(q_ref[...].astype(F32))
    newkv = _pad_rows(newkv_ref[...], LANE)
    newk128, newv128 = newkv[:, 0:LANE], newkv[:, LANE:2 * LANE]
    for kv in range(KV_B):
        qs = jnp.concatenate(q_heads[kv * G_B:(kv + 1) * G_B], axis=0).astype(BF16)
        k_t = jnp.concatenate([r[kv * 64:(kv + 1) * 64, :].astype(BF16) for r in kv_refs], axis=1)
        v_t = jnp.concatenate([r[(KV_B + kv) * 64:(KV_B + kv + 1) * 64, :].astype(BF16) for r in kv_refs],
                              axis=1)
        v_t2 = jnp.concatenate([v_t, v_t], axis=0)
        newk = (newk128 if kv == 0 else pltpu.roll(newk128, 64, 1))[:, :64].astype(BF16)
        s = jnp.concatenate([jnp.dot(qs, k_t, preferred_element_type=F32), _nt_dot(qs, newk)], axis=1) + bias
        m = jnp.maximum(jnp.max(s, axis=1, keepdims=True), M_FLOOR)
        p = jnp.exp2(s - m)
        l = jnp.sum(p, axis=1, keepdims=True)
        pb = p.astype(BF16)
        o_new = jnp.dot(pb[:, n_past:], newv128.astype(BF16), preferred_element_type=F32)
        swapped = pltpu.roll(o_new, 64, 1)
        lo4 = jnp.concatenate([lo] * G_B, axis=0)
        o_new = jnp.where(lo4, o_new, swapped) if kv == 0 else jnp.where(lo4, swapped, o_new)
        o = (_nt_dot(pb[:, :n_past], v_t2) + o_new) / jnp.where(l > 0.0, l, 1.0)
        for pr in range(G_B // 2):
            even = o[2 * pr * S_ROWS:(2 * pr + 1) * S_ROWS]
            odd = o[(2 * pr + 1) * S_ROWS:(2 * pr + 2) * S_ROWS]
            c0 = (kv * (G_B // 2) + pr) * LANE
            o_ref[:, c0:c0 + LANE] = jnp.where(lo, even, odd).astype(o_ref.dtype)


def _attn_b_sample(pt, iq, iw, bq, newkv, newik, cache_kv_t, cache_ik_t, layer, n_new, k_sel):
    n_seq = bq.shape[0]
    n_pages = pt.shape[0] // n_seq
    page = cache_kv_t.shape[3]
    seq = lambda width: pl.BlockSpec((None, S_ROWS, width), lambda s, pt: (s, 0, 0))
    grid_spec = pltpu.PrefetchScalarGridSpec(
        num_scalar_prefetch=1,
        grid=(n_seq,),
        in_specs=[seq(512), seq(LANE), seq(512), seq(256), seq(LANE)]
        + _page_specs((2 * KV_B * HD_B, page), layer, n_pages) + _page_specs((D_I, page), layer, n_pages),
        out_specs=seq(512),
    )
    return pl.pallas_call(
        functools.partial(_attn_b_sample_kernel, n_pages=n_pages, page=page, n_new=n_new, k_sel=k_sel),
        grid_spec=grid_spec,
        out_shape=jax.ShapeDtypeStruct((n_seq, S_ROWS, 512), BF16),
        compiler_params=_params(("parallel",)),
        name="attn_b_sample",
    )(pt, iq, iw, bq, newkv, newik, *([cache_kv_t] * n_pages), *([cache_ik_t] * n_pages))


def kernel(x_prompt, x_sample, cache_kv_a, cache_kv_b, cache_idx_b, state_conv_c, state_conv_ffn, page_table, meta_tokens, norm_mix, w_in, lam_q1, lam_k1, lam_q2, lam_k2, subln_a, w_out_a, w_out_b, conv_c_w, conv_c_b, ln_c_g, ln_c_b, w_out_c, w_o, norm_ffn, w_up, conv_f_w, conv_f_b, w_down, norm_final):
    depth = w_in.shape[0]
    n_prompt, seq, _ = x_prompt.shape
    t_len = seq + N_META
    pad = (-t_len) % ROW_TILE
    tp = t_len + pad
    n_seq, n_new, _ = x_sample.shape
    n_pool, page = cache_kv_a.shape[1:3]
    n_pages = page_table.shape[1]
    past_len = n_pages * page
    k_sel_p = min(TOPK_MAX, seq // 4)
    k_sel_s = min(TOPK_MAX, (past_len + n_new) // 4)
    assert n_new <= S_ROWS and ROW_TILE % n_new == 0 and (n_seq * n_new) % ROW_TILE == 0
    assert tp % ROW_TILE == 0 and KEY_TILE == ROW_TILE and page == LANE

    xp = jnp.concatenate([jnp.zeros((n_prompt, pad, D_MODEL), F32),
                          jnp.broadcast_to(meta_tokens[None], (n_prompt, N_META, D_MODEL)), x_prompt], axis=1)
    xp = xp.reshape(n_prompt * tp, D_MODEL)
    xs = x_sample.reshape(n_seq * n_new, D_MODEL)
    tab_p = _rotary_tables(jnp.arange(tp, dtype=I32) - pad)
    tab_s = _rotary_tables(past_len + (jnp.arange(ROW_TILE, dtype=I32) % n_new))
    valid_p = (jnp.arange(tp) >= pad).astype(F32)[:, None]
    valid_s = jnp.ones((ROW_TILE, 1), F32)
    valid_p_flat = jnp.tile(valid_p, (n_prompt, 1))
    pt_flat = page_table.reshape(-1).astype(I32)
    cache_a = cache_kv_a.reshape(depth, n_pool, page * 2 * KV_A, 2 * HD_A)
    cache_b_t = jnp.transpose(cache_kv_b, (0, 1, 3, 4, 5, 2)).reshape(depth, n_pool, 2 * KV_B * HD_B, page)
    cache_i_t = jnp.transpose(cache_idx_b, (0, 1, 3, 2))

    row1 = lambda a: a.reshape(1, -1)
    pad_seq = lambda a: jnp.pad(a.reshape(n_seq, n_new, -1), ((0, 0), (0, S_ROWS - n_new), (0, 0)))
    unpad_seq = lambda a: a[:, :n_new].reshape(n_seq * n_new, -1)
    per_seq = lambda a: a.reshape(n_prompt, tp, a.shape[-1])
    tiles_t = lambda a: a.reshape(n_prompt, tp // KEY_TILE, ROW_TILE, KEY_TILE)

    outs_p = [[] for _ in range(5)]
    outs_s = [[] for _ in range(5)]
    for l in range(depth):
        lam_init = 0.8 - 0.6 * math.exp(-0.3 * l)
        lam = (jnp.exp(jnp.sum(lam_q1[l] * lam_k1[l])) - jnp.exp(jnp.sum(lam_q2[l] * lam_k2[l]))
               + lam_init).reshape(1).astype(F32)
        out_scale = 1.0 - lam_init
        last = l == depth - 1
        w_in_l, w_in_t = _prep_w_in(w_in[l])
        wa, wb, wc, wo = (w_out_a[l].astype(BF16), w_out_b[l].astype(BF16), w_out_c[l].astype(BF16),
                          w_o[l].astype(BF16))
        wu, wd = w_up[l].astype(BF16), w_down[l].astype(BF16)
        ffn_w = (row1(norm_ffn[l]), wu, conv_f_w[l], row1(conv_f_b[l]), wd, row1(norm_final))
        conv_w = (conv_c_w[l], row1(conv_c_b[l]), row1(ln_c_g[l]), row1(ln_c_b[l]))

        z = _in_proj(xp, row1(norm_mix[l]), w_in_l, w_in_t, *tab_p, valid_p, tp // ROW_TILE)
        ya = _attn_a_prompt(lam, per_seq(z["aq"]), per_seq(z["kab"]),
                            z["avt"].reshape(n_prompt, tp // A_KEY_TILE, ROW_TILE, A_KEY_TILE),
                            subln_a[l].reshape(-1, 1), pad, out_scale)
        yb = _attn_b_prompt(per_seq(z["iq"]), z["iwt"], per_seq(z["bq"]), per_seq(z["ikkb"]), per_seq(z["bkk"]),
                            tiles_t(z["bvt"]), pad, k_sel_p)
        yc = _conv_c_prompt(z["u"], *conv_w)
        xp = _merge(ya.reshape(-1, 512), yb.reshape(-1, 512), yc, z["gz"], xp, wa, wb, wc, wo)
        xp, tails = _ffn(xp, *ffn_w, (valid_p_flat,), carried=True, final_norm=last)
        outs_p[0].append(per_seq(z["kva"])[:, pad:].reshape(n_prompt, t_len, 2, KV_A, 2 * HD_A))
        outs_p[1].append(per_seq(z["kvb"])[:, pad:].reshape(n_prompt, t_len, 2, KV_B, HD_B))
        outs_p[2].append(per_seq(z["ikk"])[:, pad:, :D_I])
        outs_p[3].append(per_seq(z["u"])[:, tp - (CONV_C - 1):])
        tails = tails.reshape(n_prompt, tp // ROW_TILE, TAIL, 2 * D_FF)
        outs_p[4].append(tails[:, -1, TAIL - (CONV_F - 1):])

        z = _in_proj(xs, row1(norm_mix[l]), w_in_l, w_in_t, *tab_s, valid_s, 1)
        ya = _attn_a_sample(pt_flat, lam, pad_seq(z["aq"]), pad_seq(z["kva"]), row1(subln_a[l]), cache_a, l,
                            n_new, out_scale)
        yb = _attn_b_sample(pt_flat, pad_seq(z["iq"]), pad_seq(z["iw"]), pad_seq(z["bq"]), pad_seq(z["kvb"]),
                            pad_seq(z["ikk"]), cache_b_t, cache_i_t, l, n_new, k_sel_s)
        u_seq = z["u"].reshape(n_seq, n_new, D_C)
        yc = _conv_c_sample(jnp.swapaxes(state_conv_c[l], 0, 1), jnp.swapaxes(u_seq, 0, 1), *conv_w)
        yc = jnp.swapaxes(yc, 0, 1).reshape(n_seq * n_new, D_C)
        xs = _merge(unpad_seq(ya), unpad_seq(yb), yc, z["gz"], xs, wa, wb, wc, wo)
        st_f = state_conv_ffn[l]
        zrow = jnp.zeros_like(st_f[:, :1])
        h1 = jnp.concatenate([st_f[:, 1:2]] + [zrow] * (n_new - 1), axis=1).reshape(n_seq * n_new, 2 * D_FF)
        h2 = jnp.concatenate([st_f[:, 0:1], st_f[:, 1:2]] + [zrow] * (n_new - 2), axis=1).reshape(
            n_seq * n_new, 2 * D_FF)
        xs, up = _ffn(xs, *ffn_w, (h1, h2), carried=False, final_norm=last, n_new=n_new)
        outs_s[0].append(z["kva"].reshape(n_seq, n_new, 2, KV_A, 2 * HD_A))
        outs_s[1].append(z["kvb"].reshape(n_seq, n_new, 2, KV_B, HD_B))
        outs_s[2].append(z["ikk"][:, :D_I].reshape(n_seq, n_new, D_I))
        outs_s[3].append(jnp.concatenate([state_conv_c[l], u_seq], axis=1)[:, -(CONV_C - 1):])
        up_ext = jnp.concatenate([st_f, up.reshape(n_seq, n_new, 2 * D_FF)], axis=1)
        outs_s[4].append(up_ext[:, -(CONV_F - 1):])

    y_prompt = xp.reshape(n_prompt, tp, D_MODEL)[:, pad + N_META:]
    y_sample = xs.reshape(n_seq, n_new, D_MODEL)
    stk = lambda xs_: jnp.stack(xs_, axis=0)
    return (y_prompt, y_sample, *[stk(o) for o in outs_p], *[stk(o) for o in outs_s])
```

```python
import functools
import math

import jax
import jax.numpy as jnp
from jax import lax
from jax.experimental import pallas as pl
from jax.experimental.pallas import tpu as pltpu

D_MODEL = 1024
N_META = 16
H_A, KV_A, HD_A = 4, 2, 64
G_A = H_A // KV_A
VD_A = 2 * HD_A
H_B, KV_B, HD_B = 8, 2, 64
G_B = H_B // KV_B
H_I, D_I = 8, 64
TOPK_MAX = 256
D_C, CONV_C = 512, 31
D_FF, CONV_F = 2816, 3
ROPE_THETA = 500000.0
ROT_HALF = 8
EPS = 1e-6
LN_EPS = 1e-5

LANE = 128
SUBLANE = 8
ROW_TILE = 256
Q_TILE = 128
KEY_TILE = 256
A_KEY_TILE = 256
FF_CHUNK = 256
FFN_TILE = 512
VMEM_LIMIT = 56 * 1024 * 1024

LOG2E = 1.4426950408889634
MASKED = -1e30
M_FLOOR = -5e29
NO_SCORE = -3.0e38
INT_MIN = -(2 ** 31)
INT_MAX = 2 ** 31 - 1
KEY_F_MAX = 2147483392.0
SEARCH_CAP = 80

F32 = jnp.float32
BF16 = jnp.bfloat16
I32 = jnp.int32

C_AQ, C_KVA, C_BQ, C_BKK, C_KVB, C_IQ, C_IKK, C_IW, C_CA, C_CG, C_GZ = (
    0, 512, 1024, 1536, 1792, 2048, 2560, 2688, 2816, 3328, 3840)
N_PROJ = C_GZ + 3 * D_MODEL
R_AVT, R_BVT, R_IWT, N_PROJ_T = 0, 256, 512, 528


def _params(sem):
    return pltpu.CompilerParams(dimension_semantics=sem, vmem_limit_bytes=VMEM_LIMIT)


def _const_spec(shape):
    nd = len(shape)
    return pl.BlockSpec(shape, lambda *_: (0,) * nd)


def _rms(x, g):
    return x * lax.rsqrt(jnp.mean(x * x, axis=-1, keepdims=True) + EPS) * g


def _nt_dot(a, b):
    return lax.dot_general(a, b, (((1,), (1,)), ((), ())), preferred_element_type=F32)


def _lane_ids(shape):
    return lax.broadcasted_iota(I32, shape, len(shape) - 1)


def _row_ids(shape):
    return lax.broadcasted_iota(I32, shape, 0)


def _in_proj_kernel(x_ref, g_ref, w_ref, wt_ref, cos_ref, sa_ref, sb_ref, valid_ref,
                    aq_ref, kva_ref, kab_ref, avt_ref, bq_ref, bkk_ref, bvt_ref, kvb_ref,
                    iq_ref, ikk_ref, ikkb_ref, iw_ref, iwt_ref, u_ref, gz_ref):
    xb = _rms(x_ref[...], g_ref[...]).astype(BF16)
    cos, sa, sb = cos_ref[...], sa_ref[...], sb_ref[...]

    def proj(start, width):
        return jnp.dot(xb, w_ref[:, start:start + width], preferred_element_type=F32)

    def proj_t(start, height):
        return _nt_dot(wt_ref[start:start + height, :], xb)

    def rot(z):
        return z * cos + pltpu.roll(z, LANE - ROT_HALF, 1) * sa + pltpu.roll(z, ROT_HALF, 1) * sb

    def groups(z, n_rot):
        n = z.shape[1] // LANE
        return [rot(z[:, k * LANE:(k + 1) * LANE]) if k < n_rot else z[:, k * LANE:(k + 1) * LANE]
                for k in range(n)]

    def put(ref, k, val):
        ref[:, k * LANE:(k + 1) * LANE] = val.astype(ref.dtype)

    for k, zk in enumerate(groups(proj(C_AQ, 512), 4)):
        put(aq_ref, k, zk * (HD_A ** -0.5 * LOG2E))
    for k, zk in enumerate(groups(proj(C_KVA, 512), 2)):
        put(kva_ref, k, zk)
        if k < 2:
            put(kab_ref, k, zk)
    avt = proj_t(R_AVT, 256).astype(BF16)
    for h in range(ROW_TILE // A_KEY_TILE):
        avt_ref[h] = avt[:, h * A_KEY_TILE:(h + 1) * A_KEY_TILE]
    for k, zk in enumerate(groups(proj(C_BQ, 512), 4)):
        put(bq_ref, k, zk * (HD_B ** -0.5 * LOG2E))
    for k, zk in enumerate(groups(proj(C_BKK, 256), 2)):
        put(bkk_ref, k, zk)
    bvt_ref[...] = proj_t(R_BVT, 256).astype(BF16)
    for k, zk in enumerate(groups(proj(C_KVB, 256), 1)):
        put(kvb_ref, k, zk)
    for k, zk in enumerate(groups(proj(C_IQ, 512), 4)):
        put(iq_ref, k, zk * (D_I ** -0.5))
    ikk = rot(proj(C_IKK, LANE))
    ikk_ref[...] = ikk
    ikkb_ref[...] = ikk.astype(BF16)
    iw_ref[...] = proj(C_IW, LANE) * (H_I ** -0.5)
    iwt_ref[...] = proj_t(R_IWT, 2 * SUBLANE)[:H_I] * (H_I ** -0.5)
    u_ref[...] = proj(C_CA, D_C) * jax.nn.sigmoid(proj(C_CG, D_C)) * valid_ref[...]
    gz_ref[...] = proj(C_GZ, 3 * D_MODEL)


IN_PROJ_OUTS = ("aq", "kva", "kab", "avt", "bq", "bkk", "bvt", "kvb", "iq", "ikk", "ikkb", "iw", "iwt", "u", "gz")


def _in_proj(x, g, w, wt, cos, sa, sb, valid, table_tiles):
    n = x.shape[0]
    nt = n // ROW_TILE
    row = lambda width: pl.BlockSpec((ROW_TILE, width), lambda i: (i, 0))
    tab = pl.BlockSpec((ROW_TILE, LANE), lambda i: (i % table_tiles, 0))
    rows = lambda width, dt: (row(width), jax.ShapeDtypeStruct((n, width), dt))
    tile_t = (pl.BlockSpec((None, ROW_TILE, ROW_TILE), lambda i: (i, 0, 0)),
              jax.ShapeDtypeStruct((nt, ROW_TILE, ROW_TILE), BF16))
    n_a = ROW_TILE // A_KEY_TILE
    tile_a = (pl.BlockSpec((None, n_a, ROW_TILE, A_KEY_TILE), lambda i: (i, 0, 0, 0)),
              jax.ShapeDtypeStruct((nt, n_a, ROW_TILE, A_KEY_TILE), BF16))
    outs = [rows(512, BF16), rows(512, F32), rows(256, BF16), tile_a, rows(512, BF16), rows(256, BF16), tile_t,
            rows(256, F32), rows(512, BF16), rows(LANE, F32), rows(LANE, BF16), rows(LANE, F32),
            (pl.BlockSpec((H_I, ROW_TILE), lambda i: (0, i)), jax.ShapeDtypeStruct((H_I, n), F32)),
            rows(D_C, F32), rows(3 * D_MODEL, F32)]
    res = pl.pallas_call(
        _in_proj_kernel,
        grid=(nt,),
        in_specs=[row(D_MODEL), _const_spec((1, D_MODEL)), _const_spec((D_MODEL, N_PROJ)),
                  _const_spec((N_PROJ_T, D_MODEL)),
                  tab, tab, tab, pl.BlockSpec((ROW_TILE, 1), lambda i: (i % table_tiles, 0))],
        out_specs=[o[0] for o in outs],
        out_shape=[o[1] for o in outs],
        compiler_params=_params(("parallel",)),
        name="in_proj",
    )(x, g, w, wt, cos, sa, sb, valid)
    return dict(zip(IN_PROJ_OUTS, res))


def _prep_w_in(w):
    o_aq, o_ak, o_av, o_bq, o_bk, o_bv, o_iq, o_iw, o_ik, o_cin, o_gz = (
        0, 512, 768, 1024, 1536, 1664, 1792, 2304, 2312, 2376, 3400)
    bk0, bk1 = w[:, o_bk:o_bk + 64], w[:, o_bk + 64:o_bk + 128]
    bv0, bv1 = w[:, o_bv:o_bv + 64], w[:, o_bv + 64:o_bv + 128]
    ik = w[:, o_ik:o_ik + D_I]
    iw = w[:, o_iw:o_iw + H_I]
    cols = [w[:, o_aq:o_aq + 512], w[:, o_ak:o_ak + 256], w[:, o_av:o_av + 256], w[:, o_bq:o_bq + 512],
            bk0, bk0, bk1, bk1,
            w[:, o_bk:o_bk + 128], w[:, o_bv:o_bv + 128],
            w[:, o_iq:o_iq + 512], ik, ik,
            iw, jnp.zeros((w.shape[0], LANE - H_I), w.dtype),
            w[:, o_cin:o_cin + 2 * D_C], w[:, o_gz:o_gz + 3 * D_MODEL]]
    out = jnp.concatenate(cols, axis=1).astype(BF16)
    cols_t = [w[:, o_av:o_av + 256], bv0, bv0, bv1, bv1, iw,
              jnp.zeros((w.shape[0], N_PROJ_T - R_IWT - H_I), w.dtype)]
    out_t = jnp.concatenate(cols_t, axis=1).T.astype(BF16)
    assert out.shape[1] == N_PROJ and out_t.shape[0] == N_PROJ_T
    return out, out_t


def _rotary_tables(pos):
    rot = 2 * ROT_HALF
    inv = jnp.power(ROPE_THETA, -jnp.arange(ROT_HALF, dtype=F32) * 2.0 / rot)
    ang = pos.astype(F32)[:, None] * inv[None, :]
    cos, sin = jnp.cos(ang), jnp.sin(ang)
    t = pos.shape[0]
    rest = 64 - rot
    one = jnp.ones((t, rest), F32)
    zero = jnp.zeros((t, rest), F32)
    zh = jnp.zeros((t, ROT_HALF), F32)
    c64 = jnp.concatenate([cos, cos, one], axis=1)
    sa64 = jnp.concatenate([-sin, zh, zero], axis=1)
    sb64 = jnp.concatenate([zh, sin, zero], axis=1)
    dup = lambda a: jnp.concatenate([a, a], axis=1)
    return dup(c64), dup(sa64), dup(sb64)


def _split_pair(pair):
    lo = _lane_ids(pair.shape) < 64
    zero = jnp.zeros_like(pair)
    return jnp.where(lo, pair, zero), jnp.where(lo, zero, pair)


def _monotone_key(score):
    b = lax.bitcast_convert_type(score, I32)
    return jnp.where(b < 0, b ^ I32(0x7FFFFFFF), b)


Q_PAIR = 2 * Q_TILE


def _pair_up(q_groups):
    return [jnp.concatenate(q_groups[n:n + 2], axis=0) for n in range(0, len(q_groups), 2)]


def _two_tile_pipeline(prefetch, consume, state, j_first, j_last, mask_middle):
    n_trips = (j_last - j_first + 2) // 2
    carry = (prefetch(0, j_first, True), state)
    segments = ((0, 1, True), (1, n_trips - 2, mask_middle), (jnp.maximum(n_trips - 2, 1), n_trips, True))
    for first, stop, masked in segments:
        def body(jj, c, masked=masked):
            t0 = j_first + 2 * jj
            aux_b = prefetch(1, t0 + 1, masked)
            st = consume(0, t0, c[0], c[1], masked)
            aux_a = prefetch(0, t0 + 2, masked)
            return aux_a, consume(1, t0 + 1, aux_b, st, masked)
        carry = lax.fori_loop(first, stop, body, carry)
    return carry[1]


def _flash_run_t(load_k, load_vt, bias_of, q_pairs, j_first, j_last, mask_middle, s_ref):
    kv_of = [kv for kv, pairs in enumerate(q_pairs) for _ in pairs]
    qs = [q for pairs in q_pairs for q in pairs]

    def prefetch(slot, j, masked):
        jc = jnp.minimum(j, j_last)
        if masked:
            bias = bias_of(j)
            bias = jnp.concatenate([bias, bias], axis=1)
        k = [load_k(jc, kv) for kv in range(len(q_pairs))]
        cmax = []
        for n, (kv, q) in enumerate(zip(kv_of, qs)):
            s = _nt_dot(k[kv], q)
            if masked:
                s = s + bias
            s_ref[slot, n] = s
            cmax.append(jnp.max(s, axis=0, keepdims=True))
        return cmax

    def step(slot, j, cmax, state, masked):
        del masked
        jc = jnp.minimum(j, j_last)
        v_t = [load_vt(jc, kv) for kv in range(len(q_pairs))]
        out = []
        for n, (kv, (m, l, acc)) in enumerate(zip(kv_of, state)):
            m_new = jnp.maximum(m, cmax[n])
            alpha = jnp.exp2(m - m_new)
            p = jnp.exp2(s_ref[slot, n] - m_new)
            l = alpha * l + jnp.sum(p, axis=0, keepdims=True)
            acc = alpha * acc + jnp.dot(v_t[kv], p.astype(BF16), preferred_element_type=F32)
            out.append((m_new, l, acc))
        return out

    state = [(jnp.full((1, Q_PAIR), M_FLOOR, F32), jnp.zeros((1, Q_PAIR), F32), jnp.zeros((LANE, Q_PAIR), F32))
             for _ in kv_of]
    state = _two_tile_pipeline(prefetch, step, state, j_first, j_last, mask_middle)
    return [acc / jnp.where(l > 0.0, l, 1.0) for _, l, acc in state]


def _attn_a_kernel(lam_ref, q_ref, k_ref, vt_ref, sub_ref, o_ref, s_ref, *, pad, out_scale):
    i = pl.program_id(1)
    q0 = i * Q_TILE

    @pl.when(q0 + Q_TILE <= pad)
    def _():
        o_ref[...] = jnp.zeros_like(o_ref)

    @pl.when(q0 + Q_TILE > pad)
    def _():
        lam = lam_ref[0]
        kt = A_KEY_TILE
        j_pad = pad // kt
        j_diag = (q0 + Q_TILE - 1) // kt
        qpos = _lane_ids((kt, Q_TILE)) + q0
        krow = _row_ids((kt, Q_TILE))
        q_pairs = []
        for kv in range(KV_A):
            q_groups = []
            for g in range(G_A):
                c0 = (kv * G_A + g) * LANE
                q_groups += list(_split_pair(q_ref[:, c0:c0 + LANE]))
            q_pairs.append(_pair_up(q_groups))

        def load_k(j, kv):
            return k_ref[pl.ds(pl.multiple_of(j * kt, kt), kt), kv * LANE:(kv + 1) * LANE]

        def load_vt(j, kv):
            return vt_ref[j, kv * LANE:(kv + 1) * LANE, :]

        def bias_of(j):
            kpos = krow + j * kt
            return jnp.where((kpos <= qpos) & (kpos >= pad), 0.0, MASKED)

        outs = _flash_run_t(load_k, load_vt, bias_of, q_pairs, j_pad, j_diag, False, s_ref)
        for kv in range(KV_A):
            o = outs[kv * G_A:(kv + 1) * G_A]
            for g in range(G_A):
                y = o[g][:, :Q_TILE] - lam * o[g][:, Q_TILE:]
                y = y * lax.rsqrt(jnp.mean(y * y, axis=0, keepdims=True) + EPS) * sub_ref[...] * out_scale
                c0 = (kv * G_A + g) * LANE
                o_ref[:, c0:c0 + LANE] = y.T.astype(o_ref.dtype)


def _attn_a_prompt(lam, aq, kab, avt, subln_col, pad, out_scale):
    b, tp, _ = aq.shape
    return pl.pallas_call(
        functools.partial(_attn_a_kernel, pad=pad, out_scale=out_scale),
        grid=(b, tp // Q_TILE),
        in_specs=[pl.BlockSpec(memory_space=pltpu.SMEM),
                  pl.BlockSpec((None, Q_TILE, 512), lambda bi, i: (bi, i, 0)),
                  pl.BlockSpec((None, tp, 256), lambda bi, i: (bi, 0, 0)),
                  pl.BlockSpec((None, tp // A_KEY_TILE, 256, A_KEY_TILE), lambda bi, i: (bi, 0, 0, 0)),
                  _const_spec((LANE, 1))],
        out_specs=pl.BlockSpec((None, Q_TILE, 512), lambda bi, i: (bi, i, 0)),
        out_shape=jax.ShapeDtypeStruct((b, tp, 512), BF16),
        scratch_shapes=[pltpu.VMEM((2, KV_A * G_A, A_KEY_TILE, Q_PAIR), F32)],
        compiler_params=_params(("parallel", "parallel")),
        name="attn_a_prompt",
    )(lam, aq, kab, avt, subln_col)


def _kth_largest_key(count_ge, k_sel, n_vis, score_min, score_max, passes_per_check):
    kf = float(k_sel)
    key_min, key_max = _monotone_key(score_min), _monotone_key(score_max)
    no_key = _monotone_key(jnp.full(n_vis.shape, NO_SCORE, F32))
    zero = jnp.zeros_like(key_min)
    one = zero + 1
    c_pos = count_ge(one)
    c_nn = count_ge(zero)
    few = n_vis <= kf
    pos = c_pos >= kf
    neg = c_nn < kf
    lo = jnp.where(pos, one, jnp.where(neg, key_min, zero))
    hi = jnp.where(pos, key_max + 1, jnp.where(neg, zero, one))
    c_lo = jnp.where(pos, c_pos, jnp.where(neg, n_vis, c_nn))
    c_hi = jnp.where(pos, 0.0, jnp.where(neg, c_nn, c_pos))
    lo = jnp.where(few, no_key + 1, lo)
    done = few | (c_lo == kf) | (hi - 1 <= lo)
    ones = jnp.ones_like(n_vis)

    def cond(st):
        it, active = st[0], st[-1]
        return (it < SEARCH_CAP) & (jnp.sum(active) > 0.0)

    def body(st):
        it, lo, hi, c_lo, c_hi, f_lo, f_hi, last, active = st
        done = active == 0.0
        g_lo = (c_lo - (kf - 0.5)) * f_lo
        g_hi = ((kf - 0.5) - c_hi) * f_hi
        frac = jnp.where(done, 0.5, g_hi / (g_lo + g_hi))
        lo_f, hi_f = lo.astype(F32), hi.astype(F32)
        t_f = jnp.clip(hi_f - (hi_f - lo_f) * frac, -KEY_F_MAX, KEY_F_MAX)
        t_mid = (lo >> 1) + (hi >> 1) + (lo & hi & 1)
        halve = (f_lo < 0.2) | (f_hi < 0.2) | (c_lo - c_hi <= 4.0)
        t = jnp.where(halve, t_mid, t_f.astype(I32))
        t = jnp.minimum(jnp.maximum(t, lo + 1), hi - 1)
        c = count_ge(t)
        up = c >= kf
        act_up = jnp.logical_and(jnp.logical_not(done), up)
        act_dn = jnp.logical_and(jnp.logical_not(done), jnp.logical_not(up))
        f_lo = jnp.where(act_dn & (last == -1), f_lo * 0.5, jnp.where(act_up, ones, f_lo))
        f_hi = jnp.where(act_up & (last == 1), f_hi * 0.5, jnp.where(act_dn, ones, f_hi))
        last = jnp.where(act_up, 1, jnp.where(act_dn, -1, last))
        lo = jnp.where(act_up, t, lo)
        c_lo = jnp.where(act_up, c, c_lo)
        hi = jnp.where(act_dn, t, hi)
        c_hi = jnp.where(act_dn, c, c_hi)
        done = done | (c_lo == kf) | (hi - 1 <= lo)
        return it + 1, lo, hi, c_lo, c_hi, f_lo, f_hi, last, jnp.where(done, 0.0, 1.0)

    def trip(s):
        for _ in range(passes_per_check):
            s = body(s)
        return s

    st = lax.while_loop(cond, trip,
                        (I32(0), lo, hi, c_lo, c_hi, ones, ones, zero, jnp.where(done, 0.0, 1.0)))
    return st[1], st[3]


def _tie_cut(count_tied_before, n_keep, n_pos_bits):
    def step(it, x):
        cand = x + lax.shift_left(I32(1), I32(n_pos_bits - 1) - it)
        return jnp.where(count_tied_before(cand) < n_keep, cand, x)

    return lax.fori_loop(0, n_pos_bits, step, jnp.zeros(n_keep.shape, I32))


def _attn_b_kernel(iq_ref, iwt_ref, q_ref, ik_ref, kk_ref, vvt_ref, o_ref, key_ref, s_ref, d_ref, *, pad, k_sel):
    i = pl.program_id(1)
    q0 = i * Q_TILE

    @pl.when(q0 + Q_TILE <= pad)
    def _():
        o_ref[...] = jnp.zeros_like(o_ref)

    @pl.when(q0 + Q_TILE > pad)
    def _():
        j_diag = (q0 + Q_TILE - 1) // KEY_TILE
        n_kb = j_diag + 1
        qpos = _lane_ids((KEY_TILE, Q_TILE)) + q0
        krow = _row_ids((KEY_TILE, Q_TILE))
        no_key = _monotone_key(jnp.full((1, 1), NO_SCORE, F32))
        def fold(a, op):
            parts = [a[r * SUBLANE:(r + 1) * SUBLANE] for r in range(KEY_TILE // SUBLANE)]
            while len(parts) > 1:
                parts = [op(parts[n], parts[n + 1]) for n in range(0, len(parts), 2)]
            return parts[0]

        parts = []
        for pr in range(H_I // 2):
            parts += list(_split_pair(iq_ref[:, pr * LANE:(pr + 1) * LANE]))
        qi = jnp.concatenate(parts, axis=0)
        iwt = iwt_ref[...]

        def dots(slot, j, masked):
            k0 = pl.multiple_of(jnp.minimum(j, j_diag) * KEY_TILE, KEY_TILE)
            d_ref[slot] = _nt_dot(ik_ref[pl.ds(k0, KEY_TILE), :], qi)
            return ()

        def score_tile(slot, j, _, carry, masked):
            kmin, kmax = carry
            sc = jnp.zeros((KEY_TILE, Q_TILE), F32)
            for h in range(H_I):
                sc = sc + iwt[h:h + 1, :] * jnp.maximum(d_ref[slot, :, h * Q_TILE:(h + 1) * Q_TILE], 0.0)
            sc = sc + 0.0
            sc_lo = sc_hi = sc
            key = _monotone_key(sc)
            if masked:
                kpos = krow + j * KEY_TILE
                vis = (kpos <= qpos) & (kpos >= pad)
                key = jnp.where(vis, key, no_key)
                sc_lo = jnp.where(vis, sc, -NO_SCORE)
                sc_hi = jnp.where(vis, sc, NO_SCORE)
            key_ref[j] = key
            return jnp.minimum(kmin, fold(sc_lo, jnp.minimum)), jnp.maximum(kmax, fold(sc_hi, jnp.maximum))

        carry = (jnp.full((SUBLANE, Q_TILE), -NO_SCORE, F32), jnp.full((SUBLANE, Q_TILE), NO_SCORE, F32))
        carry = _two_tile_pipeline(dots, score_tile, carry, 0, j_diag, False)
        sc_min = jnp.min(carry[0], axis=0, keepdims=True)
        sc_max = jnp.max(carry[1], axis=0, keepdims=True)

        key_ref[n_kb] = jnp.broadcast_to(no_key, (KEY_TILE, Q_TILE))

        def count_ge(t):
            def cbody(jj, c):
                for h in range(2):
                    c = c + fold(jnp.where(key_ref[2 * jj + h] >= t, 1.0, 0.0), jnp.add)
                return c
            c = lax.fori_loop(0, (n_kb + 1) // 2, cbody, jnp.zeros((SUBLANE, Q_TILE), F32))
            return jnp.sum(c, axis=0, keepdims=True)

        n_vis = (_lane_ids((1, Q_TILE)) + (q0 - pad + 1)).astype(F32)
        thr, c_thr = _kth_largest_key(count_ge, k_sel, n_vis, sc_min, sc_max, 2)
        tied = c_thr > float(k_sel)

        @pl.when(jnp.sum(jnp.where(tied, 1.0, 0.0)) > 0.0)
        def _():
            n_keep = float(k_sel) - count_ge(thr + 1)

            def tied_before(x):
                def cbody(j, c):
                    hit = (key_ref[j] == thr) & (krow + j * KEY_TILE < x)
                    return c + fold(jnp.where(hit, 1.0, 0.0), jnp.add)
                c = lax.fori_loop(0, n_kb, cbody, jnp.zeros((SUBLANE, Q_TILE), F32))
                return jnp.sum(c, axis=0, keepdims=True)

            x = _tie_cut(tied_before, n_keep, (key_ref.shape[0] * KEY_TILE).bit_length())

            def lower(j, _):
                key = key_ref[j]
                drop = tied & (key == thr) & (krow + j * KEY_TILE > x)
                key_ref[j] = jnp.where(drop, key - 1, key)
                return 0

            lax.fori_loop(0, n_kb, lower, 0)

        q_pairs = []
        for kv in range(KV_B):
            q_groups = []
            for pr in range(G_B // 2):
                c0 = (kv * (G_B // 2) + pr) * LANE
                q_groups += list(_split_pair(q_ref[:, c0:c0 + LANE]))
            q_pairs.append(_pair_up(q_groups))

        def load_k(j, kv):
            return kk_ref[pl.ds(pl.multiple_of(j * KEY_TILE, KEY_TILE), KEY_TILE), kv * LANE:(kv + 1) * LANE]

        def load_vt(j, kv):
            return vvt_ref[j, kv * LANE:(kv + 1) * LANE, :]

        def bias_of(j):
            thr_j = jnp.where(j <= j_diag, thr, INT_MAX)
            return jnp.where(key_ref[jnp.minimum(j, j_diag)] >= thr_j, 0.0, MASKED)

        outs = _flash_run_t(load_k, load_vt, bias_of, q_pairs, 0, j_diag, True, s_ref)
        lo = _lane_ids((Q_TILE, LANE)) < 64
        for kv in range(KV_B):
            o = outs[kv * (G_B // 2):(kv + 1) * (G_B // 2)]
            for pr in range(G_B // 2):
                pair = jnp.where(lo, o[pr][:, :Q_TILE].T, o[pr][:, Q_TILE:].T)
                c0 = (kv * (G_B // 2) + pr) * LANE
                o_ref[:, c0:c0 + LANE] = pair.astype(o_ref.dtype)


def _attn_b_prompt(iq, iwt, bq, ikkb, bkk, bvt, pad, k_sel):
    b, tp, _ = bq.shape
    nq = tp // Q_TILE
    qspec = lambda width: pl.BlockSpec((None, Q_TILE, width), lambda bi, i: (bi, i, 0))
    full = lambda width: pl.BlockSpec((None, tp, width), lambda bi, i: (bi, 0, 0))
    return pl.pallas_call(
        functools.partial(_attn_b_kernel, pad=pad, k_sel=k_sel),
        grid=(b, nq),
        in_specs=[qspec(512), pl.BlockSpec((H_I, Q_TILE), lambda bi, i: (0, bi * nq + i)), qspec(512),
                  full(LANE), full(256),
                  pl.BlockSpec((None, tp // KEY_TILE, 256, KEY_TILE), lambda bi, i: (bi, 0, 0, 0))],
        out_specs=qspec(512),
        out_shape=jax.ShapeDtypeStruct((b, tp, 512), BF16),
        scratch_shapes=[pltpu.VMEM((tp // KEY_TILE + 1, KEY_TILE, Q_TILE), I32),
                        pltpu.VMEM((2, KV_B * G_B // 2, KEY_TILE, Q_PAIR), F32),
                        pltpu.VMEM((2, KEY_TILE, H_I * Q_TILE), F32)],
        compiler_params=_params(("parallel", "parallel")),
        name="attn_b_prompt",
    )(iq, iwt, bq, ikkb, bkk, bvt)


def _ln_swish(y, g, b):
    mu = jnp.mean(y, axis=-1, keepdims=True)
    yc = y - mu
    z = yc * lax.rsqrt(jnp.mean(yc * yc, axis=-1, keepdims=True) + LN_EPS) * g + b
    return z * jax.nn.sigmoid(z)


HALO = 32


def _conv_c_prompt_kernel(prev_ref, cur_ref, w_ref, b_ref, g_ref, beta_ref, o_ref, ext_ref):
    ext_ref[0:HALO, :] = prev_ref[ROW_TILE - HALO:ROW_TILE, :]
    ext_ref[HALO:HALO + ROW_TILE, :] = cur_ref[...]
    acc = jnp.zeros((ROW_TILE, D_C), F32) + b_ref[...]
    base = HALO - (CONV_C - 1)
    for j in range(CONV_C):
        acc = acc + w_ref[j:j + 1, :] * ext_ref[base + j:base + j + ROW_TILE, :]
    o_ref[...] = _ln_swish(acc, g_ref[...], beta_ref[...]).astype(o_ref.dtype)


def _conv_c_prompt(u, w, b, g, beta):
    n = u.shape[0]
    return pl.pallas_call(
        _conv_c_prompt_kernel,
        grid=(n // ROW_TILE,),
        in_specs=[pl.BlockSpec((ROW_TILE, D_C), lambda i: (jnp.maximum(i - 1, 0), 0)),
                  pl.BlockSpec((ROW_TILE, D_C), lambda i: (i, 0)),
                  _const_spec((CONV_C, D_C)), _const_spec((1, D_C)), _const_spec((1, D_C)),
                  _const_spec((1, D_C))],
        out_specs=pl.BlockSpec((ROW_TILE, D_C), lambda i: (i, 0)),
        out_shape=jax.ShapeDtypeStruct((n, D_C), BF16),
        scratch_shapes=[pltpu.VMEM((HALO + ROW_TILE, D_C), F32)],
        compiler_params=_params(("parallel",)),
        name="conv_c_prompt",
    )(u, u, w, b, g, beta)


def _conv_c_sample_kernel(st_ref, u_ref, w_ref, b_ref, g_ref, beta_ref, o_ref):
    n_hist = st_ref.shape[0]
    for t in range(u_ref.shape[0]):
        acc = jnp.zeros(o_ref.shape[1:], F32) + b_ref[...]
        for j in range(CONV_C):
            e = t + j
            row = st_ref[e] if e < n_hist else u_ref[e - n_hist]
            acc = acc + w_ref[j:j + 1, :] * row
        o_ref[t] = _ln_swish(acc, g_ref[...], beta_ref[...]).astype(o_ref.dtype)


def _conv_c_sample(st_t, u_t, w, b, g, beta):
    n_new, n_seq, _ = u_t.shape
    return pl.pallas_call(
        _conv_c_sample_kernel,
        grid=(1,),
        in_specs=[_const_spec(st_t.shape), _const_spec(u_t.shape), _const_spec((CONV_C, D_C)),
                  _const_spec((1, D_C)), _const_spec((1, D_C)), _const_spec((1, D_C))],
        out_specs=_const_spec((n_new, n_seq, D_C)),
        out_shape=jax.ShapeDtypeStruct((n_new, n_seq, D_C), BF16),
        compiler_params=_params(("arbitrary",)),
        name="conv_c_sample",
    )(st_t, u_t, w, b, g, beta)


def _merge_kernel(ya_ref, yb_ref, yc_ref, gz_ref, x_ref, wa_ref, wb_ref, wc_ref, wo_ref, o_ref):
    merged = jnp.zeros((ROW_TILE, D_MODEL), F32)
    for k, (y_ref, w_ref) in enumerate(((ya_ref, wa_ref), (yb_ref, wb_ref), (yc_ref, wc_ref))):
        gate = jax.nn.sigmoid(gz_ref[:, k * D_MODEL:(k + 1) * D_MODEL])
        merged = merged + gate * jnp.dot(y_ref[...], w_ref[...], preferred_element_type=F32)
    o_ref[...] = x_ref[...] + jnp.dot(merged.astype(BF16), wo_ref[...], preferred_element_type=F32)


def _merge(ya, yb, yc, gz, x, wa, wb, wc, wo):
    n = x.shape[0]
    row = lambda width: pl.BlockSpec((ROW_TILE, width), lambda i: (i, 0))
    return pl.pallas_call(
        _merge_kernel,
        grid=(n // ROW_TILE,),
        in_specs=[row(512), row(512), row(D_C), row(3 * D_MODEL), row(D_MODEL),
                  _const_spec(wa.shape), _const_spec(wb.shape), _const_spec(wc.shape), _const_spec(wo.shape)],
        out_specs=row(D_MODEL),
        out_shape=jax.ShapeDtypeStruct((n, D_MODEL), F32),
        compiler_params=_params(("parallel",)),
        name="merge",
    )(ya, yb, yc, gz, x, wa, wb, wc, wo)


TAIL = 8


def _ffn_kernel(*refs, carried, final_norm, n_new):
    if carried:
        (x_ref, g_ref, wu_ref, cw_ref, cb_ref, wd_ref, gf_ref, valid_ref,
         o_ref, tail_ref, carry_ref) = refs
    else:
        (x_ref, g_ref, wu_ref, cw_ref, cb_ref, wd_ref, gf_ref, h1_ref, h2_ref,
         o_ref, up_ref) = refs
    x = x_ref[...]
    tm = x.shape[0]
    xb = _rms(x, g_ref[...]).astype(BF16)
    row = _row_ids((tm, 1))

    if carried:
        @pl.when(pl.program_id(0) == 0)
        def _():
            carry_ref[...] = jnp.zeros_like(carry_ref)
        valid = valid_ref[...]
    else:
        t_in_seq = row % n_new

    def conv(c0):
        up = jnp.dot(xb, wu_ref[:, c0:c0 + FF_CHUNK], preferred_element_type=F32)
        if carried:
            up = up * valid
            old = carry_ref[:, c0:c0 + FF_CHUNK]
            p1 = jnp.where(row == 0, old[TAIL - 1:TAIL], pltpu.roll(up, 1, 0))
            p2 = jnp.where(row == 0, old[TAIL - 2:TAIL - 1],
                           jnp.where(row == 1, old[TAIL - 1:TAIL], pltpu.roll(up, 2, 0)))
            carry_ref[:, c0:c0 + FF_CHUNK] = up[tm - TAIL:]
            for r in range(tm // ROW_TILE):
                tail_ref[r, :, c0:c0 + FF_CHUNK] = up[(r + 1) * ROW_TILE - TAIL:(r + 1) * ROW_TILE]
        else:
            p1 = jnp.where(t_in_seq < 1, h1_ref[:, c0:c0 + FF_CHUNK], pltpu.roll(up, 1, 0))
            p2 = jnp.where(t_in_seq < 2, h2_ref[:, c0:c0 + FF_CHUNK], pltpu.roll(up, 2, 0))
            up_ref[:, c0:c0 + FF_CHUNK] = up
        w = cw_ref[:, c0:c0 + FF_CHUNK]
        return w[0:1] * p2 + w[1:2] * p1 + w[2:3] * up + cb_ref[:, c0:c0 + FF_CHUNK]

    acc = jnp.zeros((tm, D_MODEL), F32)
    for c in range(D_FF // FF_CHUNK):
        hg = conv(c * FF_CHUNK)
        hv = conv(D_FF + c * FF_CHUNK)
        h = (hg * jax.nn.sigmoid(hg) * hv).astype(BF16)
        acc = acc + jnp.dot(h, wd_ref[c * FF_CHUNK:(c + 1) * FF_CHUNK, :], preferred_element_type=F32)
    y = x + acc
    o_ref[...] = _rms(y, gf_ref[...]) if final_norm else y


def _ffn(x, g, wu, cw, cb, wd, gf, extra, carried, final_norm, n_new=1):
    n = x.shape[0]
    tm = FFN_TILE if carried and n % FFN_TILE == 0 else ROW_TILE
    nt = n // tm
    row = lambda width: pl.BlockSpec((tm, width), lambda i: (i, 0))
    in_specs = [row(D_MODEL), _const_spec((1, D_MODEL)), _const_spec(wu.shape), _const_spec(cw.shape),
                _const_spec(cb.shape), _const_spec(wd.shape), _const_spec((1, D_MODEL))]
    if carried:
        halves = tm // ROW_TILE
        in_specs += [pl.BlockSpec((tm, 1), lambda i: (i, 0))]
        out_specs = [row(D_MODEL), pl.BlockSpec((None, halves, TAIL, 2 * D_FF), lambda i: (i, 0, 0, 0))]
        out_shape = [jax.ShapeDtypeStruct((n, D_MODEL), F32),
                     jax.ShapeDtypeStruct((nt, halves, TAIL, 2 * D_FF), F32)]
        scratch = [pltpu.VMEM((TAIL, 2 * D_FF), F32)]
    else:
        in_specs += [row(2 * D_FF), row(2 * D_FF)]
        out_specs = [row(D_MODEL), row(2 * D_FF)]
        out_shape = [jax.ShapeDtypeStruct((n, D_MODEL), F32), jax.ShapeDtypeStruct((n, 2 * D_FF), F32)]
        scratch = []
    return pl.pallas_call(
        functools.partial(_ffn_kernel, carried=carried, final_norm=final_norm, n_new=n_new),
        grid=(nt,),
        in_specs=in_specs, out_specs=out_specs, out_shape=out_shape, scratch_shapes=scratch,
        compiler_params=_params(("arbitrary",)),
        name="ffn_prompt" if carried else "ffn_sample",
    )(x, g, wu, cw, cb, wd, gf, *extra)


S_ROWS = 8


def _new_key_mask(shape, n_past, n_new):
    col = _lane_ids(shape)
    t = _row_ids(shape) % S_ROWS
    return (col < n_past) | ((col - n_past <= t) & (col < n_past + n_new))


def _pad_rows(a, rows):
    return jnp.concatenate([a, jnp.zeros((rows - a.shape[0], a.shape[1]), a.dtype)], axis=0)


def _page_specs(block, layer, n_pages):
    nd = len(block)
    return [pl.BlockSpec((None, None) + block,
                         lambda s, pt, j=j: (layer, pt[s * n_pages + j]) + (0,) * nd) for j in range(n_pages)]


def _attn_a_sample_kernel(pt_ref, lam_ref, q_ref, new_ref, sub_ref, *rest, n_pages, page, n_new, out_scale):
    page_refs, o_ref = rest[:n_pages], rest[n_pages]
    lam = lam_ref[0]
    n_past = n_pages * page
    q = q_ref[...].astype(F32)
    new = _pad_rows(new_ref[...], LANE)
    for kv in range(KV_A):
        parts = []
        for g in range(G_A):
            c0 = (kv * G_A + g) * LANE
            parts += list(_split_pair(q[:, c0:c0 + LANE]))
        qs = jnp.concatenate(parts, axis=0).astype(BF16)
        ksl = slice(kv * LANE, (kv + 1) * LANE)
        vsl = slice((KV_A + kv) * LANE, (KV_A + kv + 1) * LANE)
        k = jnp.concatenate([r[pl.ds(kv, page, stride=2 * KV_A), :].astype(BF16) for r in page_refs]
                            + [new[:, ksl].astype(BF16)], axis=0)
        v = jnp.concatenate([r[pl.ds(KV_A + kv, page, stride=2 * KV_A), :].astype(BF16) for r in page_refs]
                            + [new[:, vsl].astype(BF16)], axis=0)
        s = _nt_dot(qs, k)
        s = jnp.where(_new_key_mask(s.shape, n_past, n_new), s, MASKED)
        p = jnp.exp2(s - jnp.max(s, axis=1, keepdims=True))
        o = jnp.dot(p.astype(BF16), v, preferred_element_type=F32) / jnp.sum(p, axis=1, keepdims=True)
        for g in range(G_A):
            b0 = g * 2 * S_ROWS
            y = o[b0:b0 + S_ROWS] - lam * o[b0 + S_ROWS:b0 + 2 * S_ROWS]
            y = _rms(y, sub_ref[...]) * out_scale
            c0 = (kv * G_A + g) * LANE
            o_ref[:, c0:c0 + LANE] = y.astype(o_ref.dtype)


def _attn_a_sample(pt, lam, aq, new, subln, cache, layer, n_new, out_scale):
    n_seq = aq.shape[0]
    n_pages = pt.shape[0] // n_seq
    page = cache.shape[2] // (2 * KV_A)
    seq = lambda width: pl.BlockSpec((None, S_ROWS, width), lambda s, pt: (s, 0, 0))
    grid_spec = pltpu.PrefetchScalarGridSpec(
        num_scalar_prefetch=1,
        grid=(n_seq,),
        in_specs=[pl.BlockSpec(memory_space=pltpu.SMEM), seq(512), seq(512),
                  pl.BlockSpec((1, LANE), lambda s, pt: (0, 0))]
        + _page_specs((page * 2 * KV_A, LANE), layer, n_pages),
        out_specs=seq(512),
    )
    return pl.pallas_call(
        functools.partial(_attn_a_sample_kernel, n_pages=n_pages, page=page, n_new=n_new, out_scale=out_scale),
        grid_spec=grid_spec,
        out_shape=jax.ShapeDtypeStruct((n_seq, S_ROWS, 512), BF16),
        compiler_params=_params(("parallel",)),
        name="attn_a_sample",
    )(pt, lam, aq, new, subln, *([cache] * n_pages))


def _attn_b_sample_kernel(pt_ref, iq_ref, iw_ref, q_ref, newkv_ref, newik_ref, *rest,
                          n_pages, page, n_new, k_sel):
    kv_refs, ik_refs, o_ref = rest[:n_pages], rest[n_pages:2 * n_pages], rest[2 * n_pages]
    n_past = n_pages * page
    lo = _lane_ids((S_ROWS, LANE)) < 64

    def heads64(x):
        out = []
        for pr in range(x.shape[1] // LANE):
            pair = x[:, pr * LANE:(pr + 1) * LANE]
            out += [pair[:, :64], pltpu.roll(pair, 64, 1)[:, :64]]
        return out

    qi = jnp.concatenate(heads64(iq_ref[...].astype(F32)), axis=0).astype(BF16)
    ik_t = jnp.concatenate([r[...].astype(BF16) for r in ik_refs], axis=1)
    newik = _pad_rows(newik_ref[:, :D_I], LANE).astype(BF16)
    d = jnp.concatenate([jnp.dot(qi, ik_t, preferred_element_type=F32), _nt_dot(qi, newik)], axis=1)
    d = jnp.maximum(d, 0.0)
    iw = iw_ref[...]
    sc = jnp.zeros((S_ROWS, d.shape[1]), F32)
    for h in range(H_I):
        sc = sc + iw[:, h:h + 1] * d[h * S_ROWS:(h + 1) * S_ROWS]
    sc = sc + 0.0
    vis = _new_key_mask(sc.shape, n_past, n_new)
    no_key = _monotone_key(jnp.full((1, 1), NO_SCORE, F32))
    keys = jnp.where(vis, _monotone_key(sc), no_key)

    def count_ge(t):
        return jnp.sum(jnp.where(keys >= t, 1.0, 0.0), axis=1, keepdims=True)

    n_vis = jnp.sum(jnp.where(vis, 1.0, 0.0), axis=1, keepdims=True)
    sc_min = jnp.min(jnp.where(vis, sc, -NO_SCORE), axis=1, keepdims=True)
    sc_max = jnp.max(jnp.where(vis, sc, NO_SCORE), axis=1, keepdims=True)
    thr, c_thr = _kth_largest_key(count_ge, k_sel, n_vis, sc_min, sc_max, 1)
    tied = (c_thr > float(k_sel)) & (_row_ids((S_ROWS, 1)) < n_new)

    def break_ties(keys):
        n_keep = float(k_sel) - count_ge(thr + 1)
        kpos = _lane_ids(keys.shape)

        def tied_before(x):
            return jnp.sum(jnp.where((keys == thr) & (kpos < x), 1.0, 0.0), axis=1, keepdims=True)

        x = _tie_cut(tied_before, n_keep, keys.shape[1].bit_length())
        return jnp.where(tied & (keys == thr) & (kpos > x), keys - 1, keys)

    keys = lax.cond(jnp.sum(jnp.where(tied, 1.0, 0.0)) > 0.0, break_ties, lambda a: a, keys)
    bias = jnp.concatenate([jnp.where(keys >= thr, 0.0, MASKED)] * G_B, axis=0)

    q_heads = heads64(q_ref[...].astype(F32))
    newkv = _pad_rows(newkv_ref[...], LANE)
    newk128, newv128 = newkv[:, 0:LANE], newkv[:, LANE:2 * LANE]
    for kv in range(KV_B):
        qs = jnp.concatenate(q_heads[kv * G_B:(kv + 1) * G_B], axis=0).astype(BF16)
        k_t = jnp.concatenate([r[kv * 64:(kv + 1) * 64, :].astype(BF16) for r in kv_refs], axis=1)
        v_t = jnp.concatenate([r[(KV_B + kv) * 64:(KV_B + kv + 1) * 64, :].astype(BF16) for r in kv_refs],
                              axis=1)
        v_t2 = jnp.concatenate([v_t, v_t], axis=0)
        newk = (newk128 if kv == 0 else pltpu.roll(newk128, 64, 1))[:, :64].astype(BF16)
        s = jnp.concatenate([jnp.dot(qs, k_t, preferred_element_type=F32), _nt_dot(qs, newk)], axis=1) + bias
        m = jnp.maximum(jnp.max(s, axis=1, keepdims=True), M_FLOOR)
        p = jnp.exp2(s - m)
        l = jnp.sum(p, axis=1, keepdims=True)
        pb = p.astype(BF16)
        o_new = jnp.dot(pb[:, n_past:], newv128.astype(BF16), preferred_element_type=F32)
        swapped = pltpu.roll(o_new, 64, 1)
        lo4 = jnp.concatenate([lo] * G_B, axis=0)
        o_new = jnp.where(lo4, o_new, swapped) if kv == 0 else jnp.where(lo4, swapped, o_new)
        o = (_nt_dot(pb[:, :n_past], v_t2) + o_new) / jnp.where(l > 0.0, l, 1.0)
        for pr in range(G_B // 2):
            even = o[2 * pr * S_ROWS:(2 * pr + 1) * S_ROWS]
            odd = o[(2 * pr + 1) * S_ROWS:(2 * pr + 2) * S_ROWS]
            c0 = (kv * (G_B // 2) + pr) * LANE
            o_ref[:, c0:c0 + LANE] = jnp.where(lo, even, odd).astype(o_ref.dtype)


def _attn_b_sample(pt, iq, iw, bq, newkv, newik, cache_kv_t, cache_ik_t, layer, n_new, k_sel):
    n_seq = bq.shape[0]
    n_pages = pt.shape[0] // n_seq
    page = cache_kv_t.shape[3]
    seq = lambda width: pl.BlockSpec((None, S_ROWS, width), lambda s, pt: (s, 0, 0))
    grid_spec = pltpu.PrefetchScalarGridSpec(
        num_scalar_prefetch=1,
        grid=(n_seq,),
        in_specs=[seq(512), seq(LANE), seq(512), seq(256), seq(LANE)]
        + _page_specs((2 * KV_B * HD_B, page), layer, n_pages) + _page_specs((D_I, page), layer, n_pages),
        out_specs=seq(512),
    )
    return pl.pallas_call(
        functools.partial(_attn_b_sample_kernel, n_pages=n_pages, page=page, n_new=n_new, k_sel=k_sel),
        grid_spec=grid_spec,
        out_shape=jax.ShapeDtypeStruct((n_seq, S_ROWS, 512), BF16),
        compiler_params=_params(("parallel",)),
        name="attn_b_sample",
    )(pt, iq, iw, bq, newkv, newik, *([cache_kv_t] * n_pages), *([cache_ik_t] * n_pages))


def kernel(x_prompt, x_sample, cache_kv_a, cache_kv_b, cache_idx_b, state_conv_c, state_conv_ffn, page_table, meta_tokens, norm_mix, w_in, lam_q1, lam_k1, lam_q2, lam_k2, subln_a, w_out_a, w_out_b, conv_c_w, conv_c_b, ln_c_g, ln_c_b, w_out_c, w_o, norm_ffn, w_up, conv_f_w, conv_f_b, w_down, norm_final):
    depth = w_in.shape[0]
    n_prompt, seq, _ = x_prompt.shape
    t_len = seq + N_META
    pad = (-t_len) % ROW_TILE
    tp = t_len + pad
    n_seq, n_new, _ = x_sample.shape
    n_pool, page = cache_kv_a.shape[1:3]
    n_pages = page_table.shape[1]
    past_len = n_pages * page
    k_sel_p = min(TOPK_MAX, seq // 4)
    k_sel_s = min(TOPK_MAX, (past_len + n_new) // 4)
    assert n_new <= S_ROWS and ROW_TILE % n_new == 0 and (n_seq * n_new) % ROW_TILE == 0
    assert tp % ROW_TILE == 0 and KEY_TILE == ROW_TILE and page == LANE

    xp = jnp.concatenate([jnp.zeros((n_prompt, pad, D_MODEL), F32),
                          jnp.broadcast_to(meta_tokens[None], (n_prompt, N_META, D_MODEL)), x_prompt], axis=1)
    xp = xp.reshape(n_prompt * tp, D_MODEL)
    xs = x_sample.reshape(n_seq * n_new, D_MODEL)
    tab_p = _rotary_tables(jnp.arange(tp, dtype=I32) - pad)
    tab_s = _rotary_tables(past_len + (jnp.arange(ROW_TILE, dtype=I32) % n_new))
    valid_p = (jnp.arange(tp) >= pad).astype(F32)[:, None]
    valid_s = jnp.ones((ROW_TILE, 1), F32)
    valid_p_flat = jnp.tile(valid_p, (n_prompt, 1))
    pt_flat = page_table.reshape(-1).astype(I32)
    cache_a = cache_kv_a.reshape(depth, n_pool, page * 2 * KV_A, 2 * HD_A)
    cache_b_t = jnp.transpose(cache_kv_b, (0, 1, 3, 4, 5, 2)).reshape(depth, n_pool, 2 * KV_B * HD_B, page)
    cache_i_t = jnp.transpose(cache_idx_b, (0, 1, 3, 2))

    row1 = lambda a: a.reshape(1, -1)
    pad_seq = lambda a: jnp.pad(a.reshape(n_seq, n_new, -1), ((0, 0), (0, S_ROWS - n_new), (0, 0)))
    unpad_seq = lambda a: a[:, :n_new].reshape(n_seq * n_new, -1)
    per_seq = lambda a: a.reshape(n_prompt, tp, a.shape[-1])
    tiles_t = lambda a: a.reshape(n_prompt, tp // KEY_TILE, ROW_TILE, KEY_TILE)

    outs_p = [[] for _ in range(5)]
    outs_s = [[] for _ in range(5)]
    for l in range(depth):
        lam_init = 0.8 - 0.6 * math.exp(-0.3 * l)
        lam = (jnp.exp(jnp.sum(lam_q1[l] * lam_k1[l])) - jnp.exp(jnp.sum(lam_q2[l] * lam_k2[l]))
               + lam_init).reshape(1).astype(F32)
        out_scale = 1.0 - lam_init
        last = l == depth - 1
        w_in_l, w_in_t = _prep_w_in(w_in[l])
        wa, wb, wc, wo = (w_out_a[l].astype(BF16), w_out_b[l].astype(BF16), w_out_c[l].astype(BF16),
                          w_o[l].astype(BF16))
        wu, wd = w_up[l].astype(BF16), w_down[l].astype(BF16)
        ffn_w = (row1(norm_ffn[l]), wu, conv_f_w[l], row1(conv_f_b[l]), wd, row1(norm_final))
        conv_w = (conv_c_w[l], row1(conv_c_b[l]), row1(ln_c_g[l]), row1(ln_c_b[l]))

        z = _in_proj(xp, row1(norm_mix[l]), w_in_l, w_in_t, *tab_p, valid_p, tp // ROW_TILE)
        ya = _attn_a_prompt(lam, per_seq(z["aq"]), per_seq(z["kab"]),
                            z["avt"].reshape(n_prompt, tp // A_KEY_TILE, ROW_TILE, A_KEY_TILE),
                            subln_a[l].reshape(-1, 1), pad, out_scale)
        yb = _attn_b_prompt(per_seq(z["iq"]), z["iwt"], per_seq(z["bq"]), per_seq(z["ikkb"]), per_seq(z["bkk"]),
                            tiles_t(z["bvt"]), pad, k_sel_p)
        yc = _conv_c_prompt(z["u"], *conv_w)
        xp = _merge(ya.reshape(-1, 512), yb.reshape(-1, 512), yc, z["gz"], xp, wa, wb, wc, wo)
        xp, tails = _ffn(xp, *ffn_w, (valid_p_flat,), carried=True, final_norm=last)
        outs_p[0].append(per_seq(z["kva"])[:, pad:].reshape(n_prompt, t_len, 2, KV_A, 2 * HD_A))
        outs_p[1].append(per_seq(z["kvb"])[:, pad:].reshape(n_prompt, t_len, 2, KV_B, HD_B))
        outs_p[2].append(per_seq(z["ikk"])[:, pad:, :D_I])
        outs_p[3].append(per_seq(z["u"])[:, tp - (CONV_C - 1):])
        tails = tails.reshape(n_prompt, tp // ROW_TILE, TAIL, 2 * D_FF)
        outs_p[4].append(tails[:, -1, TAIL - (CONV_F - 1):])

        z = _in_proj(xs, row1(norm_mix[l]), w_in_l, w_in_t, *tab_s, valid_s, 1)
        ya = _attn_a_sample(pt_flat, lam, pad_seq(z["aq"]), pad_seq(z["kva"]), row1(subln_a[l]), cache_a, l,
                            n_new, out_scale)
        yb = _attn_b_sample(pt_flat, pad_seq(z["iq"]), pad_seq(z["iw"]), pad_seq(z["bq"]), pad_seq(z["kvb"]),
                            pad_seq(z["ikk"]), cache_b_t, cache_i_t, l, n_new, k_sel_s)
        u_seq = z["u"].reshape(n_seq, n_new, D_C)
        yc = _conv_c_sample(jnp.swapaxes(state_conv_c[l], 0, 1), jnp.swapaxes(u_seq, 0, 1), *conv_w)
        yc = jnp.swapaxes(yc, 0, 1).reshape(n_seq * n_new, D_C)
        xs = _merge(unpad_seq(ya), unpad_seq(yb), yc, z["gz"], xs, wa, wb, wc, wo)
        st_f = state_conv_ffn[l]
        zrow = jnp.zeros_like(st_f[:, :1])
        h1 = jnp.concatenate([st_f[:, 1:2]] + [zrow] * (n_new - 1), axis=1).reshape(n_seq * n_new, 2 * D_FF)
        h2 = jnp.concatenate([st_f[:, 0:1], st_f[:, 1:2]] + [zrow] * (n_new - 2), axis=1).reshape(
            n_seq * n_new, 2 * D_FF)
        xs, up = _ffn(xs, *ffn_w, (h1, h2), carried=False, final_norm=last, n_new=n_new)
        outs_s[0].append(z["kva"].reshape(n_seq, n_new, 2, KV_A, 2 * HD_A))
        outs_s[1].append(z["kvb"].reshape(n_seq, n_new, 2, KV_B, HD_B))
        outs_s[2].append(z["ikk"][:, :D_I].reshape(n_seq, n_new, D_I))
        outs_s[3].append(jnp.concatenate([state_conv_c[l], u_seq], axis=1)[:, -(CONV_C - 1):])
        up_ext = jnp.concatenate([st_f, up.reshape(n_seq, n_new, 2 * D_FF)], axis=1)
        outs_s[4].append(up_ext[:, -(CONV_F - 1):])

    y_prompt = xp.reshape(n_prompt, tp, D_MODEL)[:, pad + N_META:]
    y_sample = xs.reshape(n_seq, n_new, D_MODEL)
    stk = lambda xs_: jnp.stack(xs_, axis=0)
    return (y_prompt, y_sample, *[stk(o) for o in outs_p], *[stk(o) for o in outs_s])
```
